```python
import jax, jax.numpy as jnp
from jax import lax
import numpy as np

D_MODEL = 1024
BATCH = 8
SEQ = 4096
DEPTH = 1

MIX_WIDTH = D_MODEL
POOL_WIDTH = MIX_WIDTH // 2
POOL_WINDOWS = (2, 4, 8, 16)
N_POOL_GROUPS = len(POOL_WINDOWS)
POOL_GROUP = POOL_WIDTH // N_POOL_GROUPS
NA_WIDTH = MIX_WIDTH - POOL_WIDTH
NA_HEAD_DIM = 64
NA_HEADS = NA_WIDTH // NA_HEAD_DIM
GRID_W = 64
WIN_ROWS = 8
WIN_COLS = 16
COL_BLOCK = 16
COL_BAND = 2 * WIN_COLS
N_COL_BLOCKS = GRID_W // COL_BLOCK
D_FF = 4 * D_MODEL
IN_WIDTH = POOL_WIDTH + 3 * NA_WIDTH
EPS = 1e-6

kernel_name = "hybrid_pool_neighbourhood_attn_block"


def rmsnorm(x, g):
    x32 = x.astype(jnp.float32)
    y = x32 * lax.rsqrt(jnp.mean(jnp.square(x32), axis=-1, keepdims=True) + EPS)
    return y.astype(x.dtype) * g


def pool_mixer(u, w_pool, pool_scale):
    B, T, _ = u.shape
    u32 = u.astype(jnp.float32)
    cs = jnp.concatenate([jnp.zeros((B, 1, POOL_WIDTH), jnp.float32),
                          jnp.cumsum(u32, axis=1)], axis=1)
    t = jnp.arange(T)
    outs = []
    for gi, w in enumerate(POOL_WINDOWS):
        sl = slice(gi * POOL_GROUP, (gi + 1) * POOL_GROUP)
        lo = jnp.clip(t - w // 2, 0, T)
        hi = jnp.clip(t - w // 2 + w, 0, T)
        cs_g = cs[:, :, sl]
        win_sum = jnp.take(cs_g, hi, axis=1) - jnp.take(cs_g, lo, axis=1)
        mean = win_sum / (hi - lo).astype(jnp.float32)[None, :, None]
        outs.append(mean - u32[:, :, sl])
    pooled = jnp.stack(outs, axis=2).astype(u.dtype)
    mixed = jnp.einsum('btgc,gcd->btgd', pooled, w_pool)
    return mixed.reshape(B, T, POOL_WIDTH) * pool_scale


def neighbourhood_attention(q, k, v, rpb):
    B, T, H, dh = q.shape
    rows = T // GRID_W
    kr = min(WIN_ROWS, rows)
    scale = dh ** -0.5

    def to_grid(a):
        return a.reshape(B, rows, GRID_W, H, dh).transpose(0, 3, 1, 2, 4)

    q_g, k_g, v_g = to_grid(q), to_grid(k), to_grid(v)

    cols = np.arange(GRID_W)
    c0 = np.clip(cols - WIN_COLS // 2, 0, GRID_W - WIN_COLS)
    band_start = np.clip(np.arange(N_COL_BLOCKS) * COL_BLOCK - WIN_COLS // 2, 0, GRID_W - COL_BAND)
    key_cols = band_start[:, None] + np.arange(COL_BAND)
    q_cols = (np.arange(N_COL_BLOCKS) * COL_BLOCK)[:, None] + np.arange(COL_BLOCK)
    qc0 = c0[q_cols][:, :, None]
    kc = key_cols[:, None, :]
    col_ok = (kc >= qc0) & (kc < qc0 + WIN_COLS)
    dc_idx = np.clip(kc - q_cols[:, :, None] + WIN_COLS - 1, 0, 2 * WIN_COLS - 2)
    bias_c = rpb[:, :, dc_idx]
    col_ok_b = jnp.asarray(col_ok)[:, :, None, :]

    def row_block(r):
        r0 = jnp.clip(r - kr // 2, 0, rows - kr)
        k_rows = lax.dynamic_slice_in_dim(k_g, r0, kr, axis=2)
        v_rows = lax.dynamic_slice_in_dim(v_g, r0, kr, axis=2)
        k_band = jnp.take(k_rows, key_cols, axis=3)
        v_band = jnp.take(v_rows, key_cols, axis=3)
        q_row = lax.dynamic_index_in_dim(q_g, r, axis=2, keepdims=False)
        q_row = q_row.reshape(B, H, N_COL_BLOCKS, COL_BLOCK, dh)
        s = jnp.einsum('bhjqd,bhijkd->bhjqik', q_row, k_band).astype(jnp.float32) * scale
        dr_idx = r0 + jnp.arange(kr) - r + (WIN_ROWS - 1)
        bias = jnp.take(bias_c, dr_idx, axis=1).transpose(0, 2, 3, 1, 4)
        s = s + bias[None].astype(jnp.float32)
        s = jnp.where(col_ok_b, s, -jnp.inf)
        p = jax.nn.softmax(s.reshape(B, H, N_COL_BLOCKS, COL_BLOCK, kr * COL_BAND), axis=-1)
        p = p.reshape(B, H, N_COL_BLOCKS, COL_BLOCK, kr, COL_BAND).astype(v.dtype)
        o = jnp.einsum('bhjqik,bhijkd->bhjqd', p, v_band)
        return o.reshape(B, H, GRID_W, dh)

    out = lax.map(row_block, jnp.arange(rows))
    return out.transpose(1, 0, 3, 2, 4).reshape(B, T, H * dh)


def setup_inputs(seed: int = 0) -> dict:
    key = jax.random.key(seed)
    ks = jax.random.split(key, 12)
    f32 = jnp.float32
    x = jax.random.normal(ks[0], (BATCH, SEQ, D_MODEL), f32)
    norm_mix_g = 1.0 + 0.02 * jax.random.normal(ks[1], (DEPTH, D_MODEL), f32)
    w_in = jax.random.normal(ks[2], (DEPTH, D_MODEL, IN_WIDTH), f32) * D_MODEL ** -0.5
    w_pool = jax.random.normal(ks[3], (DEPTH, N_POOL_GROUPS, POOL_GROUP, POOL_GROUP), f32) * POOL_GROUP ** -0.5
    pool_scale = 1.0 + 0.1 * jax.random.normal(ks[4], (DEPTH, POOL_WIDTH), f32)
    rpb = 0.5 * jax.random.normal(ks[5], (DEPTH, NA_HEADS, 2 * WIN_ROWS - 1, 2 * WIN_COLS - 1), f32)
    w_out = jax.random.normal(ks[6], (DEPTH, MIX_WIDTH, D_MODEL), f32) * MIX_WIDTH ** -0.5
    norm_mlp_g = 1.0 + 0.02 * jax.random.normal(ks[7], (DEPTH, D_MODEL), f32)
    w_up = jax.random.normal(ks[8], (DEPTH, D_MODEL, D_FF), f32) * D_MODEL ** -0.5
    w_down = jax.random.normal(ks[9], (DEPTH, D_FF, D_MODEL), f32) * D_FF ** -0.5
    final_g = 1.0 + 0.02 * jax.random.normal(ks[10], (D_MODEL,), f32)
    return {"x": x, "norm_mix_g": norm_mix_g, "w_in": w_in, "w_pool": w_pool,
            "pool_scale": pool_scale, "rpb": rpb, "w_out": w_out,
            "norm_mlp_g": norm_mlp_g, "w_up": w_up, "w_down": w_down,
            "final_g": final_g}


def reference(x, norm_mix_g, w_in, w_pool, pool_scale, rpb, w_out,
              norm_mlp_g, w_up, w_down, final_g):
    B, T, _ = x.shape
    for l in range(DEPTH):
        h = rmsnorm(x, norm_mix_g[l])
        proj = h @ w_in[l]
        u = proj[..., :POOL_WIDTH]
        q, k, v = jnp.split(proj[..., POOL_WIDTH:], 3, axis=-1)
        q = q.reshape(B, T, NA_HEADS, NA_HEAD_DIM)
        k = k.reshape(B, T, NA_HEADS, NA_HEAD_DIM)
        v = v.reshape(B, T, NA_HEADS, NA_HEAD_DIM)
        a = pool_mixer(u, w_pool[l], pool_scale[l])
        b = neighbourhood_attention(q, k, v, rpb[l])
        x = x + jnp.concatenate([a, b], axis=-1) @ w_out[l]
        h = rmsnorm(x, norm_mlp_g[l])
        x = x + jnp.square(jax.nn.relu(h @ w_up[l])) @ w_down[l]
    return rmsnorm(x, final_g)
```

```python
import functools

import jax
import jax.numpy as jnp
import numpy as np
from jax import lax
from jax.experimental import pallas as pl
from jax.experimental.pallas import tpu as pltpu

D_MODEL = 1024
POOL_WIDTH = 512
POOL_WINDOWS = (2, 4, 8, 16)
POOL_GROUP = 128
NA_WIDTH = 512
HEAD_DIM = 64
N_HEADS = 8
GRID_W = 64
WIN_ROWS = 8
WIN_COLS = 16
D_FF = 4096
EPS = 1e-6

HEADS_PER_GROUP = 4
GROUP_WIDTH = HEADS_PER_GROUP * HEAD_DIM
N_GROUPS = N_HEADS // HEADS_PER_GROUP
KEYS_PER_ROW_WINDOW = WIN_ROWS * GRID_W
MASK_VALUE = -1e30

TOKEN_TILE = 512
ROWS_PER_TILE = TOKEN_TILE // GRID_W
HALO = 16
MAX_HALF_WINDOW = max(POOL_WINDOWS) // 2
FF_CHUNK = 1024

V7X_VMEM_LIMIT_BYTES = 56 * 1024 * 1024


def _rmsnorm_f32(x, g):
    ms = jnp.mean(jnp.square(x), axis=-1, keepdims=True)
    return x * lax.rsqrt(ms + EPS) * g


def _in_proj_kernel(x_ref, g_ref, w_ref, u_ref, q_ref, k_ref, v_ref):
    h = _rmsnorm_f32(x_ref[...], g_ref[...]).astype(jnp.bfloat16)
    proj = jnp.dot(h, w_ref[...], preferred_element_type=jnp.float32)
    u_ref[...] = proj[:, :POOL_WIDTH].astype(jnp.bfloat16)
    q = proj[:, POOL_WIDTH:POOL_WIDTH + NA_WIDTH] * (HEAD_DIM ** -0.5)
    q_ref[...] = q.astype(jnp.bfloat16)
    k_ref[...] = proj[:, POOL_WIDTH + NA_WIDTH:POOL_WIDTH + 2 * NA_WIDTH].astype(jnp.bfloat16)
    v_ref[...] = proj[:, POOL_WIDTH + 2 * NA_WIDTH:].astype(jnp.bfloat16)


def _in_proj(x, g, w_in_bf16):
    B, T, D = x.shape
    n_t = T // TOKEN_TILE
    tile = lambda width: pl.BlockSpec((None, TOKEN_TILE, width), lambda b, t: (b, t, 0))
    out = jax.ShapeDtypeStruct((B, T, POOL_WIDTH), jnp.bfloat16)
    return pl.pallas_call(
        _in_proj_kernel,
        grid=(B, n_t),
        in_specs=[
            tile(D),
            pl.BlockSpec((1, D), lambda b, t: (0, 0)),
            pl.BlockSpec(w_in_bf16.shape, lambda b, t: (0, 0)),
        ],
        out_specs=[tile(POOL_WIDTH)] * 4,
        out_shape=[out] * 4,
        compiler_params=pltpu.CompilerParams(
            dimension_semantics=("arbitrary", "arbitrary"),
            vmem_limit_bytes=V7X_VMEM_LIMIT_BYTES),
        name="in_proj",
    )(x, g, w_in_bf16)


def _mixers_kernel(x_ref, u_ref, u_prev_ref, u_next_ref, q_ref, k_ref, v_ref,
                   band_ref, wpool_ref, pscale_ref, bias_ref, wout_ref,
                   o_ref, mix_ref):
    tb = pl.program_id(1)
    n_tb = pl.num_programs(1)
    seq_len = n_tb * TOKEN_TILE

    u_prev = jnp.where(tb > 0, u_prev_ref[...], jnp.zeros_like(u_prev_ref))
    u_next = jnp.where(tb < n_tb - 1, u_next_ref[...], jnp.zeros_like(u_next_ref))
    u_mid = u_ref[...]
    u_halo = jnp.concatenate([u_prev, u_mid, u_next], axis=0)
    t_glob = tb * TOKEN_TILE + lax.broadcasted_iota(jnp.int32, (TOKEN_TILE, 1), 0)
    for gi, w in enumerate(POOL_WINDOWS):
        sl = slice(gi * POOL_GROUP, (gi + 1) * POOL_GROUP)
        win_sum = jnp.dot(band_ref[gi], u_halo[:, sl], preferred_element_type=jnp.float32)
        lo = jnp.maximum(t_glob - w // 2, 0)
        hi = jnp.minimum(t_glob - w // 2 + w, seq_len)
        mean = win_sum / (hi - lo).astype(jnp.float32)
        pooled = (mean - u_mid[:, sl].astype(jnp.float32)).astype(jnp.bfloat16)
        mixed = jnp.dot(pooled, wpool_ref[gi], preferred_element_type=jnp.float32)
        mix_ref[:, sl] = (mixed * pscale_ref[:, sl]).astype(jnp.bfloat16)

    n_rows = n_tb * ROWS_PER_TILE
    kr = min(WIN_ROWS, n_rows)
    lane_head = lax.broadcasted_iota(jnp.int32, (GRID_W, GROUP_WIDTH), 1) // HEAD_DIM

    def row_body(rl, carry):
        r = tb * ROWS_PER_TILE + rl
        r0 = jnp.clip(r - kr // 2, 0, n_rows - kr)
        dr0 = r0 - r + (WIN_ROWS - 1)
        q_start = pl.multiple_of(rl * GRID_W, GRID_W)
        k_start = pl.multiple_of(r0 * GRID_W, GRID_W)
        for g in range(N_GROUPS):
            gs = slice(g * GROUP_WIDTH, (g + 1) * GROUP_WIDTH)
            q_row = q_ref[pl.ds(q_start, GRID_W), gs]
            q_bd = jnp.concatenate(
                [jnp.where(lane_head == h, q_row, jnp.zeros_like(q_row))
                 for h in range(HEADS_PER_GROUP)], axis=0)
            k_rows = k_ref[pl.ds(k_start, KEYS_PER_ROW_WINDOW), gs]
            v_rows = v_ref[pl.ds(k_start, KEYS_PER_ROW_WINDOW), gs]
            s = lax.dot_general(q_bd, k_rows, (((1,), (1,)), ((), ())),
                                preferred_element_type=jnp.float32)
            bias = jnp.concatenate(
                [bias_ref[dr0 + 2 * j, gs, :] for j in range(WIN_ROWS // 2)], axis=1)
            s = s + bias
            m = jnp.max(s, axis=-1, keepdims=True)
            p = jnp.exp(s - m)
            l = jnp.sum(p, axis=-1, keepdims=True)
            o = jnp.dot(p.astype(jnp.bfloat16), v_rows,
                        preferred_element_type=jnp.float32)
            o = o / l
            og = jnp.zeros((GRID_W, GROUP_WIDTH), jnp.float32)
            for h in range(HEADS_PER_GROUP):
                og = jnp.where(lane_head == h, o[h * GRID_W:(h + 1) * GRID_W, :], og)
            mix_ref[pl.ds(q_start, GRID_W),
                    POOL_WIDTH + g * GROUP_WIDTH:POOL_WIDTH + (g + 1) * GROUP_WIDTH] = (
                        og.astype(jnp.bfloat16))
        return carry

    lax.fori_loop(0, ROWS_PER_TILE, row_body, 0)

    y = jnp.dot(mix_ref[...], wout_ref[...], preferred_element_type=jnp.float32)
    o_ref[...] = x_ref[...] + y


def _mixers(x, u, q, k, v, band, w_pool_bf16, pool_scale, bias_tab, w_out_bf16):
    B, T, D = x.shape
    n_t = T // TOKEN_TILE
    halo_per_tile = TOKEN_TILE // HALO
    n_halo_blocks = T // HALO
    tile = lambda width: pl.BlockSpec((None, TOKEN_TILE, width), lambda b, t: (b, t, 0))
    whole_seq = pl.BlockSpec((None, T, NA_WIDTH), lambda b, t: (b, 0, 0))
    const = lambda a: pl.BlockSpec(a.shape, lambda b, t: (0,) * a.ndim)
    return pl.pallas_call(
        _mixers_kernel,
        grid=(B, n_t),
        in_specs=[
            tile(D),
            tile(POOL_WIDTH),
            pl.BlockSpec((None, HALO, POOL_WIDTH),
                         lambda b, t: (b, jnp.maximum(t * halo_per_tile - 1, 0), 0)),
            pl.BlockSpec((None, HALO, POOL_WIDTH),
                         lambda b, t: (b, jnp.minimum((t + 1) * halo_per_tile, n_halo_blocks - 1), 0)),
            tile(NA_WIDTH),
            whole_seq,
            whole_seq,
            const(band), const(w_pool_bf16), const(pool_scale), const(bias_tab), const(w_out_bf16),
        ],
        out_specs=tile(D),
        out_shape=jax.ShapeDtypeStruct((B, T, D), jnp.float32),
        scratch_shapes=[pltpu.VMEM((TOKEN_TILE, D), jnp.bfloat16)],
        compiler_params=pltpu.CompilerParams(
            dimension_semantics=("arbitrary", "arbitrary"),
            vmem_limit_bytes=V7X_VMEM_LIMIT_BYTES),
        name="mixers",
    )(x, u, u, u, q, k, v, band, w_pool_bf16, pool_scale, bias_tab, w_out_bf16)


def _mlp_kernel(x_ref, g_ref, wup_ref, wdown_ref, gf_ref, o_ref):
    x = x_ref[...]
    h = _rmsnorm_f32(x, g_ref[...]).astype(jnp.bfloat16)
    acc = x
    for c in range(D_FF // FF_CHUNK):
        cs = slice(c * FF_CHUNK, (c + 1) * FF_CHUNK)
        up = jnp.dot(h, wup_ref[:, cs], preferred_element_type=jnp.float32)
        act = jnp.square(jnp.maximum(up, 0.0)).astype(jnp.bfloat16)
        acc = acc + jnp.dot(act, wdown_ref[cs, :], preferred_element_type=jnp.float32)
    o_ref[...] = _rmsnorm_f32(acc, gf_ref[...])


def _mlp(x, g, w_up_bf16, w_down_bf16, g_final):
    B, T, D = x.shape
    n_t = T // TOKEN_TILE
    tile = pl.BlockSpec((None, TOKEN_TILE, D), lambda b, t: (b, t, 0))
    const = lambda a: pl.BlockSpec(a.shape, lambda b, t: (0,) * a.ndim,
                                   pipeline_mode=pl.Buffered(1))
    return pl.pallas_call(
        _mlp_kernel,
        grid=(B, n_t),
        in_specs=[tile, const(g), const(w_up_bf16), const(w_down_bf16), const(g_final)],
        out_specs=tile,
        out_shape=jax.ShapeDtypeStruct((B, T, D), jnp.float32),
        compiler_params=pltpu.CompilerParams(
            dimension_semantics=("arbitrary", "arbitrary"),
            vmem_limit_bytes=V7X_VMEM_LIMIT_BYTES),
        name="mlp",
    )(x, g, w_up_bf16, w_down_bf16, g_final)


def _pool_band():
    t = np.arange(TOKEN_TILE)[:, None]
    s = np.arange(TOKEN_TILE + 2 * HALO)[None, :] - HALO
    bands = []
    for w in POOL_WINDOWS:
        assert w // 2 <= HALO
        bands.append((s >= t - w // 2) & (s < t - w // 2 + w))
    return jnp.asarray(np.stack(bands), dtype=jnp.bfloat16)


def _bias_table(rpb):
    qc = np.arange(GRID_W)[:, None]
    kc = np.arange(GRID_W)[None, :]
    c0 = np.clip(qc - WIN_COLS // 2, 0, GRID_W - WIN_COLS)
    ok = (kc >= c0) & (kc < c0 + WIN_COLS)
    dc = np.clip(kc - qc + WIN_COLS - 1, 0, 2 * WIN_COLS - 2)
    tab = jnp.where(jnp.asarray(ok)[None, None], rpb[:, :, dc], MASK_VALUE)
    tab = tab.transpose(1, 0, 2, 3).reshape(2 * WIN_ROWS - 1, N_HEADS * GRID_W, GRID_W)
    return jnp.concatenate([tab[:-1], tab[1:]], axis=-1)


def kernel(x, norm_mix_g, w_in, w_pool, pool_scale, rpb, w_out, norm_mlp_g, w_up, w_down, final_g):
    bf16 = jnp.bfloat16
    band = _pool_band()
    depth = w_in.shape[0]
    for l in range(depth):
        u, q, k, v = _in_proj(x, norm_mix_g[l][None, :], w_in[l].astype(bf16))
        x = _mixers(x, u, q, k, v, band, w_pool[l].astype(bf16), pool_scale[l][None, :],
                    _bias_table(rpb[l]), w_out[l].astype(bf16))
        last = l == depth - 1
        assert last, "only depth 1 is fused end to end"
        x = _mlp(x, norm_mlp_g[l][None, :], w_up[l].astype(bf16), w_down[l].astype(bf16),
                 final_g[None, :])
    return x
```

```python
import functools

import jax
import jax.numpy as jnp
import numpy as np
from jax import lax
from jax.experimental import pallas as pl
from jax.experimental.pallas import tpu as pltpu

D_MODEL = 1024
POOL_WIDTH = 512
POOL_WINDOWS = (2, 4, 8, 16)
POOL_GROUP = 128
NA_WIDTH = 512
HEAD_DIM = 64
N_HEADS = 8
GRID_W = 64
WIN_ROWS = 8
WIN_COLS = 16
D_FF = 4096
EPS = 1e-6

HEADS_PER_GROUP = 4
GROUP_WIDTH = HEADS_PER_GROUP * HEAD_DIM
N_GROUPS = N_HEADS // HEADS_PER_GROUP
KEYS_PER_ROW_WINDOW = WIN_ROWS * GRID_W
MASK_VALUE = -1e30

TOKEN_TILE = 512
ROWS_PER_TILE = TOKEN_TILE // GRID_W
HALO = 16
MAX_HALF_WINDOW = max(POOL_WINDOWS) // 2
FF_CHUNK = 1024

V7X_VMEM_LIMIT_BYTES = 56 * 1024 * 1024


def _rmsnorm_f32(x, g):
    ms = jnp.mean(jnp.square(x), axis=-1, keepdims=True)
    return x * lax.rsqrt(ms + EPS) * g


def _in_proj_kernel(x_ref, g_ref, w_ref, u_ref, q_ref, k_ref, v_ref):
    h = _rmsnorm_f32(x_ref[...], g_ref[...]).astype(jnp.bfloat16)
    proj = jnp.dot(h, w_ref[...], preferred_element_type=jnp.float32)
    u_ref[...] = proj[:, :POOL_WIDTH].astype(jnp.bfloat16)
    q = proj[:, POOL_WIDTH:POOL_WIDTH + NA_WIDTH] * (HEAD_DIM ** -0.5)
    q_ref[...] = q.astype(jnp.bfloat16)
    k_ref[...] = proj[:, POOL_WIDTH + NA_WIDTH:POOL_WIDTH + 2 * NA_WIDTH].astype(jnp.bfloat16)
    v_ref[...] = proj[:, POOL_WIDTH + 2 * NA_WIDTH:].astype(jnp.bfloat16)


def _in_proj(x, g, w_in_bf16):
    B, T, D = x.shape
    n_t = T // TOKEN_TILE
    tile = lambda width: pl.BlockSpec((None, TOKEN_TILE, width), lambda b, t: (b, t, 0))
    out = jax.ShapeDtypeStruct((B, T, POOL_WIDTH), jnp.bfloat16)
    return pl.pallas_call(
        _in_proj_kernel,
        grid=(B, n_t),
        in_specs=[
            tile(D),
            pl.BlockSpec((1, D), lambda b, t: (0, 0)),
            pl.BlockSpec(w_in_bf16.shape, lambda b, t: (0, 0)),
        ],
        out_specs=[tile(POOL_WIDTH)] * 4,
        out_shape=[out] * 4,
        compiler_params=pltpu.CompilerParams(
            dimension_semantics=("arbitrary", "arbitrary"),
            vmem_limit_bytes=V7X_VMEM_LIMIT_BYTES),
        name="in_proj",
    )(x, g, w_in_bf16)


def _mixers_kernel(x_ref, u_ref, u_prev_ref, u_next_ref, q_ref, k_ref, v_ref,
                   band_ref, wpool_ref, pscale_ref, bias_ref, wout_ref,
                   o_ref, mix_ref):
    tb = pl.program_id(1)
    n_tb = pl.num_programs(1)
    seq_len = n_tb * TOKEN_TILE

    u_prev = jnp.where(tb > 0, u_prev_ref[...], jnp.zeros_like(u_prev_ref))
    u_next = jnp.where(tb < n_tb - 1, u_next_ref[...], jnp.zeros_like(u_next_ref))
    u_mid = u_ref[...]
    u_halo = jnp.concatenate([u_prev, u_mid, u_next], axis=0)
    t_glob = tb * TOKEN_TILE + lax.broadcasted_iota(jnp.int32, (TOKEN_TILE, 1), 0)
    for gi, w in enumerate(POOL_WINDOWS):
        sl = slice(gi * POOL_GROUP, (gi + 1) * POOL_GROUP)
        win_sum = jnp.dot(band_ref[gi], u_halo[:, sl], preferred_element_type=jnp.float32)
        lo = jnp.maximum(t_glob - w // 2, 0)
        hi = jnp.minimum(t_glob - w // 2 + w, seq_len)
        mean = win_sum / (hi - lo).astype(jnp.float32)
        pooled = (mean - u_mid[:, sl].astype(jnp.float32)).astype(jnp.bfloat16)
        mixed = jnp.dot(pooled, wpool_ref[gi], preferred_element_type=jnp.float32)
        mix_ref[:, sl] = (mixed * pscale_ref[:, sl]).astype(jnp.bfloat16)

    n_rows = n_tb * ROWS_PER_TILE
    kr = min(WIN_ROWS, n_rows)
    lane_head = lax.broadcasted_iota(jnp.int32, (GRID_W, GROUP_WIDTH), 1) // HEAD_DIM

    def row_body(rl, carry):
        r = tb * ROWS_PER_TILE + rl
        r0 = jnp.clip(r - kr // 2, 0, n_rows - kr)
        dr0 = r0 - r + (WIN_ROWS - 1)
        q_start = pl.multiple_of(rl * GRID_W, GRID_W)
        k_start = pl.multiple_of(r0 * GRID_W, GRID_W)
        for g in range(N_GROUPS):
            gs = slice(g * GROUP_WIDTH, (g + 1) * GROUP_WIDTH)
            q_row = q_ref[pl.ds(q_start, GRID_W), gs]
            q_bd = jnp.concatenate(
                [jnp.where(lane_head == h, q_row, jnp.zeros_like(q_row))
                 for h in range(HEADS_PER_GROUP)], axis=0)
            k_rows = k_ref[pl.ds(k_start, KEYS_PER_ROW_WINDOW), gs]
            v_rows = v_ref[pl.ds(k_start, KEYS_PER_ROW_WINDOW), gs]
            s = lax.dot_general(q_bd, k_rows, (((1,), (1,)), ((), ())),
                                preferred_element_type=jnp.float32)
            bias = jnp.concatenate(
                [bias_ref[dr0 + 2 * j, gs, :] for j in range(WIN_ROWS // 2)], axis=1)
            s = s + bias
            m = jnp.max(s, axis=-1, keepdims=True)
            p = jnp.exp(s - m)
            l = jnp.sum(p, axis=-1, keepdims=True)
            o = jnp.dot(p.astype(jnp.bfloat16), v_rows,
                        preferred_element_type=jnp.float32)
            o = o / l
            og = jnp.zeros((GRID_W, GROUP_WIDTH), jnp.float32)
            for h in range(HEADS_PER_GROUP):
                og = jnp.where(lane_head == h, o[h * GRID_W:(h + 1) * GRID_W, :], og)
            mix_ref[pl.ds(q_start, GRID_W),
                    POOL_WIDTH + g * GROUP_WIDTH:POOL_WIDTH + (g + 1) * GROUP_WIDTH] = (
                        og.astype(jnp.bfloat16))
        return carry

    lax.fori_loop(0, ROWS_PER_TILE, row_body, 0, unroll=True)

    y = jnp.dot(mix_ref[...], wout_ref[...], preferred_element_type=jnp.float32)
    o_ref[...] = x_ref[...] + y


def _mixers(x, u, q, k, v, band, w_pool_bf16, pool_scale, bias_tab, w_out_bf16):
    B, T, D = x.shape
    n_t = T // TOKEN_TILE
    halo_per_tile = TOKEN_TILE // HALO
    n_halo_blocks = T // HALO
    tile = lambda width: pl.BlockSpec((None, TOKEN_TILE, width), lambda b, t: (b, t, 0))
    whole_seq = pl.BlockSpec((None, T, NA_WIDTH), lambda b, t: (b, 0, 0))
    const = lambda a: pl.BlockSpec(a.shape, lambda b, t: (0,) * a.ndim)
    return pl.pallas_call(
        _mixers_kernel,
        grid=(B, n_t),
        in_specs=[
            tile(D),
            tile(POOL_WIDTH),
            pl.BlockSpec((None, HALO, POOL_WIDTH),
                         lambda b, t: (b, jnp.maximum(t * halo_per_tile - 1, 0), 0)),
            pl.BlockSpec((None, HALO, POOL_WIDTH),
                         lambda b, t: (b, jnp.minimum((t + 1) * halo_per_tile, n_halo_blocks - 1), 0)),
            tile(NA_WIDTH),
            whole_seq,
            whole_seq,
            const(band), const(w_pool_bf16), const(pool_scale), const(bias_tab), const(w_out_bf16),
        ],
        out_specs=tile(D),
        out_shape=jax.ShapeDtypeStruct((B, T, D), jnp.float32),
        scratch_shapes=[pltpu.VMEM((TOKEN_TILE, D), jnp.bfloat16)],
        compiler_params=pltpu.CompilerParams(
            dimension_semantics=("arbitrary", "arbitrary"),
            vmem_limit_bytes=V7X_VMEM_LIMIT_BYTES),
        name="mixers",
    )(x, u, u, u, q, k, v, band, w_pool_bf16, pool_scale, bias_tab, w_out_bf16)


def _mlp_kernel(x_ref, g_ref, wup_ref, wdown_ref, gf_ref, o_ref):
    x = x_ref[...]
    h = _rmsnorm_f32(x, g_ref[...]).astype(jnp.bfloat16)
    acc = x
    for c in range(D_FF // FF_CHUNK):
        cs = slice(c * FF_CHUNK, (c + 1) * FF_CHUNK)
        up = jnp.dot(h, wup_ref[:, cs], preferred_element_type=jnp.float32)
        act = jnp.square(jnp.maximum(up, 0.0)).astype(jnp.bfloat16)
        acc = acc + jnp.dot(act, wdown_ref[cs, :], preferred_element_type=jnp.float32)
    o_ref[...] = _rmsnorm_f32(acc, gf_ref[...])


def _mlp(x, g, w_up_bf16, w_down_bf16, g_final):
    B, T, D = x.shape
    n_t = T // TOKEN_TILE
    tile = pl.BlockSpec((None, TOKEN_TILE, D), lambda b, t: (b, t, 0))
    const = lambda a: pl.BlockSpec(a.shape, lambda b, t: (0,) * a.ndim,
                                   pipeline_mode=pl.Buffered(1))
    return pl.pallas_call(
        _mlp_kernel,
        grid=(B, n_t),
        in_specs=[tile, const(g), const(w_up_bf16), const(w_down_bf16), const(g_final)],
        out_specs=tile,
        out_shape=jax.ShapeDtypeStruct((B, T, D), jnp.float32),
        compiler_params=pltpu.CompilerParams(
            dimension_semantics=("arbitrary", "arbitrary"),
            vmem_limit_bytes=V7X_VMEM_LIMIT_BYTES),
        name="mlp",
    )(x, g, w_up_bf16, w_down_bf16, g_final)


def _pool_band():
    t = np.arange(TOKEN_TILE)[:, None]
    s = np.arange(TOKEN_TILE + 2 * HALO)[None, :] - HALO
    bands = []
    for w in POOL_WINDOWS:
        assert w // 2 <= HALO
        bands.append((s >= t - w // 2) & (s < t - w // 2 + w))
    return jnp.asarray(np.stack(bands), dtype=jnp.bfloat16)


def _bias_table(rpb):
    qc = np.arange(GRID_W)[:, None]
    kc = np.arange(GRID_W)[None, :]
    c0 = np.clip(qc - WIN_COLS // 2, 0, GRID_W - WIN_COLS)
    ok = (kc >= c0) & (kc < c0 + WIN_COLS)
    dc = np.clip(kc - qc + WIN_COLS - 1, 0, 2 * WIN_COLS - 2)
    tab = jnp.where(jnp.asarray(ok)[None, None], rpb[:, :, dc], MASK_VALUE)
    tab = tab.transpose(1, 0, 2, 3).reshape(2 * WIN_ROWS - 1, N_HEADS * GRID_W, GRID_W)
    return jnp.concatenate([tab[:-1], tab[1:]], axis=-1)


def kernel(x, norm_mix_g, w_in, w_pool, pool_scale, rpb, w_out, norm_mlp_g, w_up, w_down, final_g):
    bf16 = jnp.bfloat16
    band = _pool_band()
    depth = w_in.shape[0]
    for l in range(depth):
        u, q, k, v = _in_proj(x, norm_mix_g[l][None, :], w_in[l].astype(bf16))
        x = _mixers(x, u, q, k, v, band, w_pool[l].astype(bf16), pool_scale[l][None, :],
                    _bias_table(rpb[l]), w_out[l].astype(bf16))
        last = l == depth - 1
        assert last, "only depth 1 is fused end to end"
        x = _mlp(x, norm_mlp_g[l][None, :], w_up[l].astype(bf16), w_down[l].astype(bf16),
                 final_g[None, :])
    return x
```

```python
import functools

import jax
import jax.numpy as jnp
import numpy as np
from jax import lax
from jax.experimental import pallas as pl
from jax.experimental.pallas import tpu as pltpu

D_MODEL = 1024
POOL_WIDTH = 512
POOL_WINDOWS = (2, 4, 8, 16)
POOL_GROUP = 128
NA_WIDTH = 512
HEAD_DIM = 64
N_HEADS = 8
GRID_W = 64
WIN_ROWS = 8
WIN_COLS = 16
D_FF = 4096
EPS = 1e-6

HEADS_PER_GROUP = 4
GROUP_WIDTH = HEADS_PER_GROUP * HEAD_DIM
N_GROUPS = N_HEADS // HEADS_PER_GROUP
KEYS_PER_ROW_WINDOW = WIN_ROWS * GRID_W
MASK_VALUE = -1e30

TOKEN_TILE = 512
ROWS_PER_TILE = TOKEN_TILE // GRID_W
DENSE_TILE = 1024
HALO = 16
MAX_HALF_WINDOW = max(POOL_WINDOWS) // 2
FF_CHUNK = 1024

V7X_VMEM_LIMIT_BYTES = 56 * 1024 * 1024


def _rmsnorm_f32(x, g):
    ms = jnp.mean(jnp.square(x), axis=-1, keepdims=True)
    return x * lax.rsqrt(ms + EPS) * g


def _in_proj_kernel(x_ref, g_ref, w_ref, u_ref, q_ref, k_ref, v_ref):
    h = _rmsnorm_f32(x_ref[...], g_ref[...]).astype(jnp.bfloat16)
    proj = jnp.dot(h, w_ref[...], preferred_element_type=jnp.float32)
    u_ref[...] = proj[:, :POOL_WIDTH].astype(jnp.bfloat16)
    q = proj[:, POOL_WIDTH:POOL_WIDTH + NA_WIDTH] * (HEAD_DIM ** -0.5)
    q_ref[...] = q.astype(jnp.bfloat16)
    k_ref[...] = proj[:, POOL_WIDTH + NA_WIDTH:POOL_WIDTH + 2 * NA_WIDTH].astype(jnp.bfloat16)
    v_ref[...] = proj[:, POOL_WIDTH + 2 * NA_WIDTH:].astype(jnp.bfloat16)


def _in_proj(x, g, w_in_bf16):
    B, T, D = x.shape
    n_t = T // DENSE_TILE
    tile = lambda width: pl.BlockSpec((None, DENSE_TILE, width), lambda b, t: (b, t, 0))
    out = jax.ShapeDtypeStruct((B, T, POOL_WIDTH), jnp.bfloat16)
    return pl.pallas_call(
        _in_proj_kernel,
        grid=(B, n_t),
        in_specs=[
            tile(D),
            pl.BlockSpec((1, D), lambda b, t: (0, 0)),
            pl.BlockSpec(w_in_bf16.shape, lambda b, t: (0, 0)),
        ],
        out_specs=[tile(POOL_WIDTH)] * 4,
        out_shape=[out] * 4,
        compiler_params=pltpu.CompilerParams(
            dimension_semantics=("arbitrary", "arbitrary"),
            vmem_limit_bytes=V7X_VMEM_LIMIT_BYTES),
        name="in_proj",
    )(x, g, w_in_bf16)


def _mixers_kernel(x_ref, u_ref, u_prev_ref, u_next_ref, q_ref, k_ref, v_ref,
                   band_ref, wpool_ref, pscale_ref, bias_ref, wout_ref,
                   o_ref, mix_ref):
    tb = pl.program_id(1)
    n_tb = pl.num_programs(1)
    seq_len = n_tb * TOKEN_TILE

    u_prev = jnp.where(tb > 0, u_prev_ref[...], jnp.zeros_like(u_prev_ref))
    u_next = jnp.where(tb < n_tb - 1, u_next_ref[...], jnp.zeros_like(u_next_ref))
    u_mid = u_ref[...]
    u_halo = jnp.concatenate([u_prev, u_mid, u_next], axis=0)
    t_glob = tb * TOKEN_TILE + lax.broadcasted_iota(jnp.int32, (TOKEN_TILE, 1), 0)
    for gi, w in enumerate(POOL_WINDOWS):
        sl = slice(gi * POOL_GROUP, (gi + 1) * POOL_GROUP)
        win_sum = jnp.dot(band_ref[gi], u_halo[:, sl], preferred_element_type=jnp.float32)
        lo = jnp.maximum(t_glob - w // 2, 0)
        hi = jnp.minimum(t_glob - w // 2 + w, seq_len)
        mean = win_sum / (hi - lo).astype(jnp.float32)
        pooled = (mean - u_mid[:, sl].astype(jnp.float32)).astype(jnp.bfloat16)
        mixed = jnp.dot(pooled, wpool_ref[gi], preferred_element_type=jnp.float32)
        mix_ref[:, sl] = (mixed * pscale_ref[:, sl]).astype(jnp.bfloat16)

    n_rows = n_tb * ROWS_PER_TILE
    kr = min(WIN_ROWS, n_rows)
    lane_head = lax.broadcasted_iota(jnp.int32, (GRID_W, GROUP_WIDTH), 1) // HEAD_DIM

    def row_body(rl, carry):
        r = tb * ROWS_PER_TILE + rl
        r0 = jnp.clip(r - kr // 2, 0, n_rows - kr)
        dr0 = r0 - r + (WIN_ROWS - 1)
        q_start = pl.multiple_of(rl * GRID_W, GRID_W)
        k_start = pl.multiple_of(r0 * GRID_W, GRID_W)
        for g in range(N_GROUPS):
            gs = slice(g * GROUP_WIDTH, (g + 1) * GROUP_WIDTH)
            q_row = q_ref[pl.ds(q_start, GRID_W), gs]
            q_bd = jnp.concatenate(
                [jnp.where(lane_head == h, q_row, jnp.zeros_like(q_row))
                 for h in range(HEADS_PER_GROUP)], axis=0)
            k_rows = k_ref[pl.ds(k_start, KEYS_PER_ROW_WINDOW), gs]
            v_rows = v_ref[pl.ds(k_start, KEYS_PER_ROW_WINDOW), gs]
            s = lax.dot_general(q_bd, k_rows, (((1,), (1,)), ((), ())),
                                preferred_element_type=jnp.float32)
            bias = jnp.concatenate(
                [bias_ref[dr0 + 2 * j, gs, :] for j in range(WIN_ROWS // 2)], axis=1)
            s = s + bias
            m = jnp.max(s, axis=-1, keepdims=True)
            p = jnp.exp(s - m)
            l = jnp.sum(p, axis=-1, keepdims=True)
            o = jnp.dot(p.astype(jnp.bfloat16), v_rows,
                        preferred_element_type=jnp.float32)
            o = o / l
            og = jnp.zeros((GRID_W, GROUP_WIDTH), jnp.float32)
            for h in range(HEADS_PER_GROUP):
                og = jnp.where(lane_head == h, o[h * GRID_W:(h + 1) * GRID_W, :], og)
            mix_ref[pl.ds(q_start, GRID_W),
                    POOL_WIDTH + g * GROUP_WIDTH:POOL_WIDTH + (g + 1) * GROUP_WIDTH] = (
                        og.astype(jnp.bfloat16))
        return carry

    lax.fori_loop(0, ROWS_PER_TILE, row_body, 0, unroll=True)

    y = jnp.dot(mix_ref[...], wout_ref[...], preferred_element_type=jnp.float32)
    o_ref[...] = x_ref[...] + y


def _mixers(x, u, q, k, v, band, w_pool_bf16, pool_scale, bias_tab, w_out_bf16):
    B, T, D = x.shape
    n_t = T // TOKEN_TILE
    halo_per_tile = TOKEN_TILE // HALO
    n_halo_blocks = T // HALO
    tile = lambda width: pl.BlockSpec((None, TOKEN_TILE, width), lambda b, t: (b, t, 0))
    whole_seq = pl.BlockSpec((None, T, NA_WIDTH), lambda b, t: (b, 0, 0))
    const = lambda a: pl.BlockSpec(a.shape, lambda b, t: (0,) * a.ndim)
    return pl.pallas_call(
        _mixers_kernel,
        grid=(B, n_t),
        in_specs=[
            tile(D),
            tile(POOL_WIDTH),
            pl.BlockSpec((None, HALO, POOL_WIDTH),
                         lambda b, t: (b, jnp.maximum(t * halo_per_tile - 1, 0), 0)),
            pl.BlockSpec((None, HALO, POOL_WIDTH),
                         lambda b, t: (b, jnp.minimum((t + 1) * halo_per_tile, n_halo_blocks - 1), 0)),
            tile(NA_WIDTH),
            whole_seq,
            whole_seq,
            const(band), const(w_pool_bf16), const(pool_scale), const(bias_tab), const(w_out_bf16),
        ],
        out_specs=tile(D),
        out_shape=jax.ShapeDtypeStruct((B, T, D), jnp.float32),
        scratch_shapes=[pltpu.VMEM((TOKEN_TILE, D), jnp.bfloat16)],
        compiler_params=pltpu.CompilerParams(
            dimension_semantics=("arbitrary", "arbitrary"),
            vmem_limit_bytes=V7X_VMEM_LIMIT_BYTES),
        name="mixers",
    )(x, u, u, u, q, k, v, band, w_pool_bf16, pool_scale, bias_tab, w_out_bf16)


def _mlp_kernel(x_ref, g_ref, wup_ref, wdown_ref, gf_ref, o_ref):
    x = x_ref[...]
    h = _rmsnorm_f32(x, g_ref[...]).astype(jnp.bfloat16)
    acc = x
    for c in range(D_FF // FF_CHUNK):
        cs = slice(c * FF_CHUNK, (c + 1) * FF_CHUNK)
        up = jnp.dot(h, wup_ref[:, cs], preferred_element_type=jnp.float32)
        act = jnp.square(jnp.maximum(up, 0.0)).astype(jnp.bfloat16)
        acc = acc + jnp.dot(act, wdown_ref[cs, :], preferred_element_type=jnp.float32)
    o_ref[...] = _rmsnorm_f32(acc, gf_ref[...])


def _mlp(x, g, w_up_bf16, w_down_bf16, g_final):
    B, T, D = x.shape
    n_t = T // DENSE_TILE
    tile = pl.BlockSpec((None, DENSE_TILE, D), lambda b, t: (b, t, 0))
    const = lambda a: pl.BlockSpec(a.shape, lambda b, t: (0,) * a.ndim,
                                   pipeline_mode=pl.Buffered(1))
    return pl.pallas_call(
        _mlp_kernel,
        grid=(B, n_t),
        in_specs=[tile, const(g), const(w_up_bf16), const(w_down_bf16), const(g_final)],
        out_specs=tile,
        out_shape=jax.ShapeDtypeStruct((B, T, D), jnp.float32),
        compiler_params=pltpu.CompilerParams(
            dimension_semantics=("arbitrary", "arbitrary"),
            vmem_limit_bytes=V7X_VMEM_LIMIT_BYTES),
        name="mlp",
    )(x, g, w_up_bf16, w_down_bf16, g_final)


def _pool_band():
    t = np.arange(TOKEN_TILE)[:, None]
    s = np.arange(TOKEN_TILE + 2 * HALO)[None, :] - HALO
    bands = []
    for w in POOL_WINDOWS:
        assert w // 2 <= HALO
        bands.append((s >= t - w // 2) & (s < t - w // 2 + w))
    return jnp.asarray(np.stack(bands), dtype=jnp.bfloat16)


def _bias_table(rpb):
    n_heads, n_dr, n_dc = rpb.shape
    qc = np.arange(GRID_W)[:, None]
    kc = np.arange(GRID_W)[None, :]
    c0 = np.clip(qc - WIN_COLS // 2, 0, GRID_W - WIN_COLS)
    ok = (kc >= c0) & (kc < c0 + WIN_COLS)
    period = 2 * GRID_W
    lead = GRID_W - WIN_COLS
    w = jnp.pad(rpb, ((0, 0), (0, 0), (lead, period - lead - n_dc)))
    skew = jnp.tile(w, (1, 1, GRID_W + 1))[..., :GRID_W * (period + 1)]
    skew = skew.reshape(n_heads, n_dr, GRID_W, period + 1)[..., :GRID_W]
    tab = jnp.flip(skew, axis=2)
    tab = jnp.where(jnp.asarray(ok)[None, None], tab, MASK_VALUE)
    tab = tab.transpose(1, 0, 2, 3).reshape(n_dr, n_heads * GRID_W, GRID_W)
    return jnp.concatenate([tab[:-1], tab[1:]], axis=-1)


def kernel(x, norm_mix_g, w_in, w_pool, pool_scale, rpb, w_out, norm_mlp_g, w_up, w_down, final_g):
    bf16 = jnp.bfloat16
    band = _pool_band()
    depth = w_in.shape[0]
    for l in range(depth):
        u, q, k, v = _in_proj(x, norm_mix_g[l][None, :], w_in[l].astype(bf16))
        x = _mixers(x, u, q, k, v, band, w_pool[l].astype(bf16), pool_scale[l][None, :],
                    _bias_table(rpb[l]), w_out[l].astype(bf16))
        last = l == depth - 1
        assert last, "only depth 1 is fused end to end"
        x = _mlp(x, norm_mlp_g[l][None, :], w_up[l].astype(bf16), w_down[l].astype(bf16),
                 final_g[None, :])
    return x
```

```python
import functools

import jax
import jax.numpy as jnp
import numpy as np
from jax import lax
from jax.experimental import pallas as pl
from jax.experimental.pallas import tpu as pltpu

D_MODEL = 1024
POOL_WIDTH = 512
POOL_WINDOWS = (2, 4, 8, 16)
POOL_GROUP = 128
NA_WIDTH = 512
HEAD_DIM = 64
N_HEADS = 8
GRID_W = 64
WIN_ROWS = 8
WIN_COLS = 16
D_FF = 4096
EPS = 1e-6

HEADS_PER_GROUP = 4
GROUP_WIDTH = HEADS_PER_GROUP * HEAD_DIM
N_GROUPS = N_HEADS // HEADS_PER_GROUP
KEYS_PER_ROW_WINDOW = WIN_ROWS * GRID_W
MASK_VALUE = -1e30

TOKEN_TILE = 512
ROWS_PER_TILE = TOKEN_TILE // GRID_W
DENSE_TILE = 1024
HALO = 16
MAX_HALF_WINDOW = max(POOL_WINDOWS) // 2
FF_CHUNK = 1024

V7X_VMEM_LIMIT_BYTES = 56 * 1024 * 1024


def _rmsnorm_f32(x, g):
    ms = jnp.mean(jnp.square(x), axis=-1, keepdims=True)
    return x * lax.rsqrt(ms + EPS) * g


def _in_proj_kernel(x_ref, g_ref, w_ref, u_ref, q_ref, k_ref, v_ref):
    h = _rmsnorm_f32(x_ref[...], g_ref[...]).astype(jnp.bfloat16)
    proj = jnp.dot(h, w_ref[...], preferred_element_type=jnp.float32)
    u_ref[...] = proj[:, :POOL_WIDTH].astype(jnp.bfloat16)
    q = proj[:, POOL_WIDTH:POOL_WIDTH + NA_WIDTH] * (HEAD_DIM ** -0.5)
    q_ref[...] = q.astype(jnp.bfloat16)
    k_ref[...] = proj[:, POOL_WIDTH + NA_WIDTH:POOL_WIDTH + 2 * NA_WIDTH].astype(jnp.bfloat16)
    v_ref[...] = proj[:, POOL_WIDTH + 2 * NA_WIDTH:].astype(jnp.bfloat16)


def _in_proj(x, g, w_in_bf16):
    B, T, D = x.shape
    n_t = T // DENSE_TILE
    tile = lambda width: pl.BlockSpec((None, DENSE_TILE, width), lambda b, t: (b, t, 0))
    out = jax.ShapeDtypeStruct((B, T, POOL_WIDTH), jnp.bfloat16)
    return pl.pallas_call(
        _in_proj_kernel,
        grid=(B, n_t),
        in_specs=[
            tile(D),
            pl.BlockSpec((1, D), lambda b, t: (0, 0)),
            pl.BlockSpec(w_in_bf16.shape, lambda b, t: (0, 0)),
        ],
        out_specs=[tile(POOL_WIDTH)] * 4,
        out_shape=[out] * 4,
        compiler_params=pltpu.CompilerParams(
            dimension_semantics=("arbitrary", "arbitrary"),
            vmem_limit_bytes=V7X_VMEM_LIMIT_BYTES),
        name="in_proj",
    )(x, g, w_in_bf16)


def _mixers_kernel(x_ref, u_ref, u_prev_ref, u_next_ref, q_ref, k_ref, v_ref,
                   wpool_ref, pscale_ref, bias_ref, wout_ref,
                   o_ref, mix_ref):
    tb = pl.program_id(1)
    n_tb = pl.num_programs(1)
    seq_len = n_tb * TOKEN_TILE

    u_prev = jnp.where(tb > 0, u_prev_ref[...], jnp.zeros_like(u_prev_ref))
    u_next = jnp.where(tb < n_tb - 1, u_next_ref[...], jnp.zeros_like(u_next_ref))
    u_halo = jnp.concatenate([u_prev, u_ref[...], u_next], axis=0).astype(jnp.float32)
    halo_len = TOKEN_TILE + 2 * HALO
    t_glob = tb * TOKEN_TILE + lax.broadcasted_iota(jnp.int32, (TOKEN_TILE, 1), 0)

    def shifted(a, s):
        return pltpu.roll(a, (-s) % halo_len, axis=0)

    pooled = []
    for gi, w in enumerate(POOL_WINDOWS):
        sl = slice(gi * POOL_GROUP, (gi + 1) * POOL_GROUP)
        x = u_halo[:, sl]
        run = x
        for k in range(w.bit_length() - 2):
            run = run + shifted(run, 2 ** k)
        win_sum = (shifted(run, -(w // 2)) + run)[HALO:HALO + TOKEN_TILE]
        lo = jnp.maximum(t_glob - w // 2, 0)
        hi = jnp.minimum(t_glob - w // 2 + w, seq_len)
        inv_count = 1.0 / (hi - lo).astype(jnp.float32)
        pooled.append((win_sum * inv_count - x[HALO:HALO + TOKEN_TILE]).astype(jnp.bfloat16))
    for pair in range(len(POOL_WINDOWS) // 2):
        sl = slice(pair * 2 * POOL_GROUP, (pair + 1) * 2 * POOL_GROUP)
        lhs = jnp.concatenate(pooled[2 * pair:2 * pair + 2], axis=1)
        mixed = jnp.dot(lhs, wpool_ref[pair], preferred_element_type=jnp.float32)
        mix_ref[:, sl] = (mixed * pscale_ref[:, sl]).astype(jnp.bfloat16)

    n_rows = n_tb * ROWS_PER_TILE
    kr = min(WIN_ROWS, n_rows)
    lane_head = lax.broadcasted_iota(jnp.int32, (GRID_W, GROUP_WIDTH), 1) // HEAD_DIM

    def row_body(rl, carry):
        r = tb * ROWS_PER_TILE + rl
        r0 = jnp.clip(r - kr // 2, 0, n_rows - kr)
        dr0 = r0 - r + (WIN_ROWS - 1)
        q_start = pl.multiple_of(rl * GRID_W, GRID_W)
        k_start = pl.multiple_of(r0 * GRID_W, GRID_W)
        for g in range(N_GROUPS):
            gs = slice(g * GROUP_WIDTH, (g + 1) * GROUP_WIDTH)
            q_row = q_ref[pl.ds(q_start, GRID_W), gs]
            q_bd = jnp.concatenate(
                [jnp.where(lane_head == h, q_row, jnp.zeros_like(q_row))
                 for h in range(HEADS_PER_GROUP)], axis=0)
            k_rows = k_ref[pl.ds(k_start, KEYS_PER_ROW_WINDOW), gs]
            v_rows = v_ref[pl.ds(k_start, KEYS_PER_ROW_WINDOW), gs]
            s = lax.dot_general(q_bd, k_rows, (((1,), (1,)), ((), ())),
                                preferred_element_type=jnp.float32)
            bias = jnp.concatenate(
                [bias_ref[dr0 + 2 * j, gs, :] for j in range(WIN_ROWS // 2)], axis=1)
            s = s + bias
            m = jnp.max(s, axis=-1, keepdims=True)
            p = jnp.exp(s - m)
            l = jnp.sum(p, axis=-1, keepdims=True)
            o = jnp.dot(p.astype(jnp.bfloat16), v_rows,
                        preferred_element_type=jnp.float32)
            o = o / l
            og = jnp.zeros((GRID_W, GROUP_WIDTH), jnp.float32)
            for h in range(HEADS_PER_GROUP):
                og = jnp.where(lane_head == h, o[h * GRID_W:(h + 1) * GRID_W, :], og)
            mix_ref[pl.ds(q_start, GRID_W),
                    POOL_WIDTH + g * GROUP_WIDTH:POOL_WIDTH + (g + 1) * GROUP_WIDTH] = (
                        og.astype(jnp.bfloat16))
        return carry

    lax.fori_loop(0, ROWS_PER_TILE, row_body, 0, unroll=True)

    y = jnp.dot(mix_ref[...], wout_ref[...], preferred_element_type=jnp.float32)
    o_ref[...] = x_ref[...] + y


def _mixers(x, u, q, k, v, w_pool_pairs, pool_scale, bias_tab, w_out_bf16):
    B, T, D = x.shape
    n_t = T // TOKEN_TILE
    halo_per_tile = TOKEN_TILE // HALO
    n_halo_blocks = T // HALO
    tile = lambda width: pl.BlockSpec((None, TOKEN_TILE, width), lambda b, t: (b, t, 0))
    whole_seq = pl.BlockSpec((None, T, NA_WIDTH), lambda b, t: (b, 0, 0))
    const = lambda a: pl.BlockSpec(a.shape, lambda b, t: (0,) * a.ndim)
    return pl.pallas_call(
        _mixers_kernel,
        grid=(B, n_t),
        in_specs=[
            tile(D),
            tile(POOL_WIDTH),
            pl.BlockSpec((None, HALO, POOL_WIDTH),
                         lambda b, t: (b, jnp.maximum(t * halo_per_tile - 1, 0), 0)),
            pl.BlockSpec((None, HALO, POOL_WIDTH),
                         lambda b, t: (b, jnp.minimum((t + 1) * halo_per_tile, n_halo_blocks - 1), 0)),
            tile(NA_WIDTH),
            whole_seq,
            whole_seq,
            const(w_pool_pairs), const(pool_scale), const(bias_tab), const(w_out_bf16),
        ],
        out_specs=tile(D),
        out_shape=jax.ShapeDtypeStruct((B, T, D), jnp.float32),
        scratch_shapes=[pltpu.VMEM((TOKEN_TILE, D), jnp.bfloat16)],
        compiler_params=pltpu.CompilerParams(
            dimension_semantics=("arbitrary", "arbitrary"),
            vmem_limit_bytes=V7X_VMEM_LIMIT_BYTES),
        name="mixers",
    )(x, u, u, u, q, k, v, w_pool_pairs, pool_scale, bias_tab, w_out_bf16)


def _mlp_kernel(x_ref, g_ref, wup_ref, wdown_ref, gf_ref, o_ref):
    x = x_ref[...]
    h = _rmsnorm_f32(x, g_ref[...]).astype(jnp.bfloat16)
    acc = x
    for c in range(D_FF // FF_CHUNK):
        cs = slice(c * FF_CHUNK, (c + 1) * FF_CHUNK)
        up = jnp.dot(h, wup_ref[:, cs], preferred_element_type=jnp.float32)
        act = jnp.square(jnp.maximum(up, 0.0)).astype(jnp.bfloat16)
        acc = acc + jnp.dot(act, wdown_ref[cs, :], preferred_element_type=jnp.float32)
    o_ref[...] = _rmsnorm_f32(acc, gf_ref[...])


def _mlp(x, g, w_up_bf16, w_down_bf16, g_final):
    B, T, D = x.shape
    n_t = T // DENSE_TILE
    tile = pl.BlockSpec((None, DENSE_TILE, D), lambda b, t: (b, t, 0))
    const = lambda a: pl.BlockSpec(a.shape, lambda b, t: (0,) * a.ndim,
                                   pipeline_mode=pl.Buffered(1))
    return pl.pallas_call(
        _mlp_kernel,
        grid=(B, n_t),
        in_specs=[tile, const(g), const(w_up_bf16), const(w_down_bf16), const(g_final)],
        out_specs=tile,
        out_shape=jax.ShapeDtypeStruct((B, T, D), jnp.float32),
        compiler_params=pltpu.CompilerParams(
            dimension_semantics=("arbitrary", "arbitrary"),
            vmem_limit_bytes=V7X_VMEM_LIMIT_BYTES),
        name="mlp",
    )(x, g, w_up_bf16, w_down_bf16, g_final)


def _pool_weight_pairs(w_pool_bf16):
    n_groups, cg, _ = w_pool_bf16.shape
    zero = jnp.zeros((cg, cg), w_pool_bf16.dtype)
    pairs = [jnp.block([[w_pool_bf16[2 * p], zero], [zero, w_pool_bf16[2 * p + 1]]])
             for p in range(n_groups // 2)]
    return jnp.stack(pairs)


def _bias_table(rpb):
    n_heads, n_dr, n_dc = rpb.shape
    qc = np.arange(GRID_W)[:, None]
    kc = np.arange(GRID_W)[None, :]
    c0 = np.clip(qc - WIN_COLS // 2, 0, GRID_W - WIN_COLS)
    ok = (kc >= c0) & (kc < c0 + WIN_COLS)
    period = 2 * GRID_W
    lead = GRID_W - WIN_COLS
    w = jnp.pad(rpb, ((0, 0), (0, 0), (lead, period - lead - n_dc)))
    skew = jnp.tile(w, (1, 1, GRID_W + 1))[..., :GRID_W * (period + 1)]
    skew = skew.reshape(n_heads, n_dr, GRID_W, period + 1)[..., :GRID_W]
    tab = jnp.flip(skew, axis=2)
    tab = jnp.where(jnp.asarray(ok)[None, None], tab, MASK_VALUE)
    tab = tab.transpose(1, 0, 2, 3).reshape(n_dr, n_heads * GRID_W, GRID_W)
    return jnp.concatenate([tab[:-1], tab[1:]], axis=-1)


def kernel(x, norm_mix_g, w_in, w_pool, pool_scale, rpb, w_out, norm_mlp_g, w_up, w_down, final_g):
    bf16 = jnp.bfloat16
    assert w_in.shape[0] == 1, "the final rmsnorm is fused into the (single) layer's MLP call"
    assert MAX_HALF_WINDOW < HALO
    u, q, k, v = _in_proj(x, norm_mix_g[0][None, :], w_in[0].astype(bf16))
    x = _mixers(x, u, q, k, v, _pool_weight_pairs(w_pool[0].astype(bf16)), pool_scale[0][None, :],
                _bias_table(rpb[0]), w_out[0].astype(bf16))
    return _mlp(x, norm_mlp_g[0][None, :], w_up[0].astype(bf16), w_down[0].astype(bf16),
                final_g[None, :])
```

```python
import functools

import jax
import jax.numpy as jnp
import numpy as np
from jax import lax
from jax.experimental import pallas as pl
from jax.experimental.pallas import tpu as pltpu

D_MODEL = 1024
POOL_WIDTH = 512
POOL_WINDOWS = (2, 4, 8, 16)
POOL_GROUP = 128
NA_WIDTH = 512
HEAD_DIM = 64
N_HEADS = 8
GRID_W = 64
WIN_ROWS = 8
WIN_COLS = 16
D_FF = 4096
EPS = 1e-6

HEADS_PER_GROUP = 4
GROUP_WIDTH = HEADS_PER_GROUP * HEAD_DIM
N_GROUPS = N_HEADS // HEADS_PER_GROUP
KEYS_PER_ROW_WINDOW = WIN_ROWS * GRID_W
MASK_VALUE = -1e30
LOG2_E = float(np.log2(np.e))
QUERY_SCALE = HEAD_DIM ** -0.5 * LOG2_E

TOKEN_TILE = 512
ROWS_PER_TILE = TOKEN_TILE // GRID_W
DENSE_TILE = 1024
HALO = 16
MAX_HALF_WINDOW = max(POOL_WINDOWS) // 2
FF_CHUNK = 1024

V7X_VMEM_LIMIT_BYTES = 56 * 1024 * 1024


def _rmsnorm_f32(x, g):
    ms = jnp.mean(jnp.square(x), axis=-1, keepdims=True)
    return x * lax.rsqrt(ms + EPS) * g


def _in_proj_kernel(x_ref, g_ref, w_ref, u_ref, q_ref, k_ref, v_ref):
    h = _rmsnorm_f32(x_ref[...], g_ref[...]).astype(jnp.bfloat16)
    proj = jnp.dot(h, w_ref[...], preferred_element_type=jnp.float32)
    u_ref[...] = proj[:, :POOL_WIDTH].astype(jnp.bfloat16)
    q = proj[:, POOL_WIDTH:POOL_WIDTH + NA_WIDTH] * QUERY_SCALE
    q_ref[...] = q.astype(jnp.bfloat16)
    k_ref[...] = proj[:, POOL_WIDTH + NA_WIDTH:POOL_WIDTH + 2 * NA_WIDTH].astype(jnp.bfloat16)
    v_ref[...] = proj[:, POOL_WIDTH + 2 * NA_WIDTH:].astype(jnp.bfloat16)


def _in_proj(x, g, w_in_bf16):
    B, T, D = x.shape
    n_t = T // DENSE_TILE
    tile = lambda width: pl.BlockSpec((None, DENSE_TILE, width), lambda b, t: (b, t, 0))
    out = jax.ShapeDtypeStruct((B, T, POOL_WIDTH), jnp.bfloat16)
    return pl.pallas_call(
        _in_proj_kernel,
        grid=(B, n_t),
        in_specs=[
            tile(D),
            pl.BlockSpec((1, D), lambda b, t: (0, 0)),
            pl.BlockSpec(w_in_bf16.shape, lambda b, t: (0, 0)),
        ],
        out_specs=[tile(POOL_WIDTH)] * 4,
        out_shape=[out] * 4,
        compiler_params=pltpu.CompilerParams(
            dimension_semantics=("arbitrary", "arbitrary"),
            vmem_limit_bytes=V7X_VMEM_LIMIT_BYTES),
        name="in_proj",
    )(x, g, w_in_bf16)


def _mixers_kernel(x_ref, u_ref, u_prev_ref, u_next_ref, q_ref, k_ref, v_ref,
                   wpool_ref, pscale_ref, bias_ref, wout_ref,
                   o_ref, attn_ref):
    tb = pl.program_id(1)
    n_tb = pl.num_programs(1)
    seq_len = n_tb * TOKEN_TILE

    n_rows = n_tb * ROWS_PER_TILE
    kr = min(WIN_ROWS, n_rows)
    lane_head = lax.broadcasted_iota(jnp.int32, (GRID_W, GROUP_WIDTH), 1) // HEAD_DIM
    head_sel = [(lane_head == h).astype(jnp.float32).astype(jnp.bfloat16) > 0
                for h in range(HEADS_PER_GROUP)]

    def row_body(rl, carry):
        r = tb * ROWS_PER_TILE + rl
        r0 = jnp.clip(r - kr // 2, 0, n_rows - kr)
        dr0 = r0 - r + (WIN_ROWS - 1)
        q_start = pl.multiple_of(rl * GRID_W, GRID_W)
        k_start = pl.multiple_of(r0 * GRID_W, GRID_W)
        for g in range(N_GROUPS):
            gs = slice(g * GROUP_WIDTH, (g + 1) * GROUP_WIDTH)
            q_row = q_ref[pl.ds(q_start, GRID_W), gs]
            q_bd = jnp.concatenate(
                [jnp.where(head_sel[h], q_row, jnp.zeros_like(q_row))
                 for h in range(HEADS_PER_GROUP)], axis=0)
            k_rows = k_ref[pl.ds(k_start, KEYS_PER_ROW_WINDOW), gs]
            v_rows = v_ref[pl.ds(k_start, KEYS_PER_ROW_WINDOW), gs]
            s = lax.dot_general(q_bd, k_rows, (((1,), (1,)), ((), ())),
                                preferred_element_type=jnp.float32)
            bias = jnp.concatenate(
                [bias_ref[dr0 + 2 * j, gs, :] for j in range(WIN_ROWS // 2)], axis=1)
            s = s + bias
            m = jnp.max(s, axis=-1, keepdims=True)
            p = jnp.exp2(s - m)
            l = jnp.sum(p, axis=-1, keepdims=True)
            o = jnp.dot(p.astype(jnp.bfloat16), v_rows,
                        preferred_element_type=jnp.float32)
            o = o / l
            og = jnp.zeros((GRID_W, GROUP_WIDTH), jnp.float32)
            for h in range(HEADS_PER_GROUP):
                og = jnp.where(lane_head == h, o[h * GRID_W:(h + 1) * GRID_W, :], og)
            attn_ref[pl.ds(q_start, GRID_W), gs] = og.astype(jnp.bfloat16)
        return carry

    lax.fori_loop(0, ROWS_PER_TILE, row_body, 0, unroll=True)

    y = jnp.dot(attn_ref[...], wout_ref[POOL_WIDTH:, :], preferred_element_type=jnp.float32)

    u_prev = jnp.where(tb > 0, u_prev_ref[...], jnp.zeros_like(u_prev_ref))
    u_next = jnp.where(tb < n_tb - 1, u_next_ref[...], jnp.zeros_like(u_next_ref))
    u_halo = jnp.concatenate([u_prev, u_ref[...], u_next], axis=0).astype(jnp.float32)
    halo_len = TOKEN_TILE + 2 * HALO
    edge = 8
    t_top = tb * TOKEN_TILE + lax.broadcasted_iota(jnp.int32, (edge, 1), 0)
    t_bot = t_top + (TOKEN_TILE - edge)

    def shifted(a, s):
        return pltpu.roll(a, (-s) % halo_len, axis=0)

    def inv_count(t, w):
        lo = jnp.maximum(t - w // 2, 0)
        hi = jnp.minimum(t - w // 2 + w, seq_len)
        return 1.0 / (hi - lo).astype(jnp.float32)

    pooled = []
    for gi, w in enumerate(POOL_WINDOWS):
        assert w // 2 <= edge
        sl = slice(gi * POOL_GROUP, (gi + 1) * POOL_GROUP)
        x = u_halo[:, sl]
        run = x
        for k in range(w.bit_length() - 2):
            run = run + shifted(run, 2 ** k)
        win_sum = (shifted(run, -(w // 2)) + run)[HALO:HALO + TOKEN_TILE]
        mean = jnp.concatenate([
            win_sum[:edge] * inv_count(t_top, w),
            win_sum[edge:TOKEN_TILE - edge] * (1.0 / w),
            win_sum[TOKEN_TILE - edge:] * inv_count(t_bot, w)], axis=0)
        pooled.append((mean - x[HALO:HALO + TOKEN_TILE]).astype(jnp.bfloat16))
    mixed = []
    for pair in range(len(POOL_WINDOWS) // 2):
        sl = slice(pair * 2 * POOL_GROUP, (pair + 1) * 2 * POOL_GROUP)
        lhs = jnp.concatenate(pooled[2 * pair:2 * pair + 2], axis=1)
        m = jnp.dot(lhs, wpool_ref[pair], preferred_element_type=jnp.float32)
        mixed.append((m * pscale_ref[:, sl]).astype(jnp.bfloat16))
    pool_out = jnp.concatenate(mixed, axis=1)

    y = y + jnp.dot(pool_out, wout_ref[:POOL_WIDTH, :], preferred_element_type=jnp.float32)
    o_ref[...] = x_ref[...] + y


def _mixers(x, u, q, k, v, w_pool_pairs, pool_scale, bias_tab, w_out_bf16):
    B, T, D = x.shape
    n_t = T // TOKEN_TILE
    halo_per_tile = TOKEN_TILE // HALO
    n_halo_blocks = T // HALO
    tile = lambda width: pl.BlockSpec((None, TOKEN_TILE, width), lambda b, t: (b, t, 0))
    whole_seq = pl.BlockSpec((None, T, NA_WIDTH), lambda b, t: (b, 0, 0))
    const = lambda a: pl.BlockSpec(a.shape, lambda b, t: (0,) * a.ndim)
    return pl.pallas_call(
        _mixers_kernel,
        grid=(B, n_t),
        in_specs=[
            tile(D),
            tile(POOL_WIDTH),
            pl.BlockSpec((None, HALO, POOL_WIDTH),
                         lambda b, t: (b, jnp.maximum(t * halo_per_tile - 1, 0), 0)),
            pl.BlockSpec((None, HALO, POOL_WIDTH),
                         lambda b, t: (b, jnp.minimum((t + 1) * halo_per_tile, n_halo_blocks - 1), 0)),
            tile(NA_WIDTH),
            whole_seq,
            whole_seq,
            const(w_pool_pairs), const(pool_scale), const(bias_tab), const(w_out_bf16),
        ],
        out_specs=tile(D),
        out_shape=jax.ShapeDtypeStruct((B, T, D), jnp.float32),
        scratch_shapes=[pltpu.VMEM((TOKEN_TILE, NA_WIDTH), jnp.bfloat16)],
        compiler_params=pltpu.CompilerParams(
            dimension_semantics=("arbitrary", "arbitrary"),
            vmem_limit_bytes=V7X_VMEM_LIMIT_BYTES),
        name="mixers",
    )(x, u, u, u, q, k, v, w_pool_pairs, pool_scale, bias_tab, w_out_bf16)


def _mlp_kernel(x_ref, g_ref, wup_ref, wdown_ref, gf_ref, o_ref):
    x = x_ref[...]
    h = _rmsnorm_f32(x, g_ref[...]).astype(jnp.bfloat16)
    acc = x
    for c in range(D_FF // FF_CHUNK):
        cs = slice(c * FF_CHUNK, (c + 1) * FF_CHUNK)
        up = jnp.dot(h, wup_ref[:, cs], preferred_element_type=jnp.float32)
        act = jnp.square(jnp.maximum(up, 0.0)).astype(jnp.bfloat16)
        acc = acc + jnp.dot(act, wdown_ref[cs, :], preferred_element_type=jnp.float32)
    o_ref[...] = _rmsnorm_f32(acc, gf_ref[...])


def _mlp(x, g, w_up_bf16, w_down_bf16, g_final):
    B, T, D = x.shape
    n_t = T // DENSE_TILE
    tile = pl.BlockSpec((None, DENSE_TILE, D), lambda b, t: (b, t, 0))
    const = lambda a: pl.BlockSpec(a.shape, lambda b, t: (0,) * a.ndim,
                                   pipeline_mode=pl.Buffered(1))
    return pl.pallas_call(
        _mlp_kernel,
        grid=(B, n_t),
        in_specs=[tile, const(g), const(w_up_bf16), const(w_down_bf16), const(g_final)],
        out_specs=tile,
        out_shape=jax.ShapeDtypeStruct((B, T, D), jnp.float32),
        compiler_params=pltpu.CompilerParams(
            dimension_semantics=("arbitrary", "arbitrary"),
            vmem_limit_bytes=V7X_VMEM_LIMIT_BYTES),
        name="mlp",
    )(x, g, w_up_bf16, w_down_bf16, g_final)


def _pool_weight_pairs(w_pool_bf16):
    n_groups, cg, _ = w_pool_bf16.shape
    zero = jnp.zeros((cg, cg), w_pool_bf16.dtype)
    pairs = [jnp.block([[w_pool_bf16[2 * p], zero], [zero, w_pool_bf16[2 * p + 1]]])
             for p in range(n_groups // 2)]
    return jnp.stack(pairs)


def _bias_table(rpb):
    n_heads, n_dr, n_dc = rpb.shape
    qc = np.arange(GRID_W)[:, None]
    kc = np.arange(GRID_W)[None, :]
    c0 = np.clip(qc - WIN_COLS // 2, 0, GRID_W - WIN_COLS)
    ok = (kc >= c0) & (kc < c0 + WIN_COLS)
    period = 2 * GRID_W
    lead = GRID_W - WIN_COLS
    w = jnp.pad(rpb, ((0, 0), (0, 0), (lead, period - lead - n_dc)))
    skew = jnp.tile(w, (1, 1, GRID_W + 1))[..., :GRID_W * (period + 1)]
    skew = skew.reshape(n_heads, n_dr, GRID_W, period + 1)[..., :GRID_W]
    tab = jnp.flip(skew, axis=2)
    tab = jnp.where(jnp.asarray(ok)[None, None], tab * LOG2_E, MASK_VALUE)
    tab = tab.transpose(1, 0, 2, 3).reshape(n_dr, n_heads * GRID_W, GRID_W)
    return jnp.concatenate([tab[:-1], tab[1:]], axis=-1)


def kernel(x, norm_mix_g, w_in, w_pool, pool_scale, rpb, w_out, norm_mlp_g, w_up, w_down, final_g):
    bf16 = jnp.bfloat16
    assert w_in.shape[0] == 1, "the final rmsnorm is fused into the (single) layer's MLP call"
    assert MAX_HALF_WINDOW < HALO
    u, q, k, v = _in_proj(x, norm_mix_g[0][None, :], w_in[0].astype(bf16))
    x = _mixers(x, u, q, k, v, _pool_weight_pairs(w_pool[0].astype(bf16)), pool_scale[0][None, :],
                _bias_table(rpb[0]), w_out[0].astype(bf16))
    return _mlp(x, norm_mlp_g[0][None, :], w_up[0].astype(bf16), w_down[0].astype(bf16),
                final_g[None, :])
```

```python
import functools

import jax
import jax.numpy as jnp
import numpy as np
from jax import lax
from jax.experimental import pallas as pl
from jax.experimental.pallas import tpu as pltpu

D_MODEL = 1024
POOL_WIDTH = 512
POOL_WINDOWS = (2, 4, 8, 16)
POOL_GROUP = 128
NA_WIDTH = 512
HEAD_DIM = 64
N_HEADS = 8
GRID_W = 64
WIN_ROWS = 8
WIN_COLS = 16
D_FF = 4096
EPS = 1e-6

HEADS_PER_GROUP = 4
GROUP_WIDTH = HEADS_PER_GROUP * HEAD_DIM
N_GROUPS = N_HEADS // HEADS_PER_GROUP
KEYS_PER_ROW_WINDOW = WIN_ROWS * GRID_W
MASK_VALUE = -1e30
LOG2_E = float(np.log2(np.e))
QUERY_SCALE = HEAD_DIM ** -0.5 * LOG2_E

TOKEN_TILE = 512
ROWS_PER_TILE = TOKEN_TILE // GRID_W
DENSE_TILE = 1024
HALO = 16
MAX_HALF_WINDOW = max(POOL_WINDOWS) // 2
FF_CHUNK = 1024

V7X_VMEM_LIMIT_BYTES = 56 * 1024 * 1024


def _rmsnorm_f32(x, g):
    ms = jnp.mean(jnp.square(x), axis=-1, keepdims=True)
    return x * lax.rsqrt(ms + EPS) * g


def _in_proj_kernel(x_ref, g_ref, w_ref, u_ref, q_ref, k_ref, v_ref):
    h = _rmsnorm_f32(x_ref[...], g_ref[...]).astype(jnp.bfloat16)
    proj = jnp.dot(h, w_ref[...], preferred_element_type=jnp.float32)
    u_ref[...] = proj[:, :POOL_WIDTH].astype(jnp.bfloat16)
    q = proj[:, POOL_WIDTH:POOL_WIDTH + NA_WIDTH] * QUERY_SCALE
    q_ref[...] = q.astype(jnp.bfloat16)
    k_ref[...] = proj[:, POOL_WIDTH + NA_WIDTH:POOL_WIDTH + 2 * NA_WIDTH].astype(jnp.bfloat16)
    v_ref[...] = proj[:, POOL_WIDTH + 2 * NA_WIDTH:].astype(jnp.bfloat16)


def _in_proj(x, g, w_in_bf16):
    B, T, D = x.shape
    n_t = T // DENSE_TILE
    tile = lambda width: pl.BlockSpec((None, DENSE_TILE, width), lambda b, t: (b, t, 0))
    out = jax.ShapeDtypeStruct((B, T, POOL_WIDTH), jnp.bfloat16)
    return pl.pallas_call(
        _in_proj_kernel,
        grid=(B, n_t),
        in_specs=[
            tile(D),
            pl.BlockSpec((1, D), lambda b, t: (0, 0)),
            pl.BlockSpec(w_in_bf16.shape, lambda b, t: (0, 0)),
        ],
        out_specs=[tile(POOL_WIDTH)] * 4,
        out_shape=[out] * 4,
        compiler_params=pltpu.CompilerParams(
            dimension_semantics=("arbitrary", "arbitrary"),
            vmem_limit_bytes=V7X_VMEM_LIMIT_BYTES),
        name="in_proj",
    )(x, g, w_in_bf16)


def _build_bias_table(rpb_ref, bias_ref):
    n_heads, n_dr, lanes = rpb_ref.shape
    assert lanes == 2 * GRID_W
    qc = lax.broadcasted_iota(jnp.int32, (GRID_W, lanes), 0)
    lane = lax.broadcasted_iota(jnp.int32, (GRID_W, lanes), 1)
    kc = lane % GRID_W
    c0 = jnp.clip(qc - WIN_COLS // 2, 0, GRID_W - WIN_COLS)
    in_window = (kc >= c0) & (kc < c0 + WIN_COLS)
    first_row = lane < GRID_W
    for h in range(n_heads):
        skew = []
        for dr in range(n_dr):
            row = jnp.broadcast_to(rpb_ref[h, dr:dr + 1, :], (GRID_W, lanes))
            skew.append(pltpu.roll(row, 0, axis=1, stride=1, stride_axis=0))
        for p in range(n_dr - 1):
            pair = jnp.where(first_row, skew[p], pltpu.roll(skew[p + 1], GRID_W, axis=1))
            bias_ref[p, h * GRID_W:(h + 1) * GRID_W, :] = jnp.where(in_window, pair * LOG2_E, MASK_VALUE)


def _mixers_kernel(x_ref, u_ref, u_prev_ref, u_next_ref, q_ref, k_ref, v_ref,
                   wpool_ref, pscale_ref, rpb_ref, wout_ref,
                   o_ref, attn_ref, bias_ref):
    tb = pl.program_id(1)
    n_tb = pl.num_programs(1)
    seq_len = n_tb * TOKEN_TILE

    @pl.when((pl.program_id(0) == 0) & (tb == 0))
    def _():
        _build_bias_table(rpb_ref, bias_ref)

    n_rows = n_tb * ROWS_PER_TILE
    kr = min(WIN_ROWS, n_rows)
    lane_head = lax.broadcasted_iota(jnp.int32, (GRID_W, GROUP_WIDTH), 1) // HEAD_DIM
    head_sel = [(lane_head == h).astype(jnp.float32).astype(jnp.bfloat16) > 0
                for h in range(HEADS_PER_GROUP)]

    def row_body(rl, carry):
        r = tb * ROWS_PER_TILE + rl
        r0 = jnp.clip(r - kr // 2, 0, n_rows - kr)
        dr0 = r0 - r + (WIN_ROWS - 1)
        q_start = pl.multiple_of(rl * GRID_W, GRID_W)
        k_start = pl.multiple_of(r0 * GRID_W, GRID_W)
        for g in range(N_GROUPS):
            gs = slice(g * GROUP_WIDTH, (g + 1) * GROUP_WIDTH)
            q_row = q_ref[pl.ds(q_start, GRID_W), gs]
            q_bd = jnp.concatenate(
                [jnp.where(head_sel[h], q_row, jnp.zeros_like(q_row))
                 for h in range(HEADS_PER_GROUP)], axis=0)
            k_rows = k_ref[pl.ds(k_start, KEYS_PER_ROW_WINDOW), gs]
            v_rows = v_ref[pl.ds(k_start, KEYS_PER_ROW_WINDOW), gs]
            s = lax.dot_general(q_bd, k_rows, (((1,), (1,)), ((), ())),
                                preferred_element_type=jnp.float32)
            bias = jnp.concatenate(
                [bias_ref[dr0 + 2 * j, gs, :] for j in range(WIN_ROWS // 2)], axis=1)
            s = s + bias
            m = jnp.max(s, axis=-1, keepdims=True)
            p = jnp.exp2(s - m)
            l = jnp.sum(p, axis=-1, keepdims=True)
            o = jnp.dot(p.astype(jnp.bfloat16), v_rows,
                        preferred_element_type=jnp.float32)
            o = o / l
            og = jnp.zeros((GRID_W, GROUP_WIDTH), jnp.float32)
            for h in range(HEADS_PER_GROUP):
                og = jnp.where(lane_head == h, o[h * GRID_W:(h + 1) * GRID_W, :], og)
            attn_ref[pl.ds(q_start, GRID_W), gs] = og.astype(jnp.bfloat16)
        return carry

    lax.fori_loop(0, ROWS_PER_TILE, row_body, 0, unroll=True)

    y = jnp.dot(attn_ref[...], wout_ref[POOL_WIDTH:, :], preferred_element_type=jnp.float32)

    u_prev = jnp.where(tb > 0, u_prev_ref[...], jnp.zeros_like(u_prev_ref))
    u_next = jnp.where(tb < n_tb - 1, u_next_ref[...], jnp.zeros_like(u_next_ref))
    u_halo = jnp.concatenate([u_prev, u_ref[...], u_next], axis=0).astype(jnp.float32)
    halo_len = TOKEN_TILE + 2 * HALO
    edge = 8
    t_top = tb * TOKEN_TILE + lax.broadcasted_iota(jnp.int32, (edge, 1), 0)
    t_bot = t_top + (TOKEN_TILE - edge)

    def shifted(a, s):
        return pltpu.roll(a, (-s) % halo_len, axis=0)

    def inv_count(t, w):
        lo = jnp.maximum(t - w // 2, 0)
        hi = jnp.minimum(t - w // 2 + w, seq_len)
        return 1.0 / (hi - lo).astype(jnp.float32)

    pooled = []
    for gi, w in enumerate(POOL_WINDOWS):
        assert w // 2 <= edge
        sl = slice(gi * POOL_GROUP, (gi + 1) * POOL_GROUP)
        x = u_halo[:, sl]
        run = x
        for k in range(w.bit_length() - 2):
            run = run + shifted(run, 2 ** k)
        win_sum = (shifted(run, -(w // 2)) + run)[HALO:HALO + TOKEN_TILE]
        mean = jnp.concatenate([
            win_sum[:edge] * inv_count(t_top, w),
            win_sum[edge:TOKEN_TILE - edge] * (1.0 / w),
            win_sum[TOKEN_TILE - edge:] * inv_count(t_bot, w)], axis=0)
        pooled.append((mean - x[HALO:HALO + TOKEN_TILE]).astype(jnp.bfloat16))
    mixed = []
    for pair in range(len(POOL_WINDOWS) // 2):
        sl = slice(pair * 2 * POOL_GROUP, (pair + 1) * 2 * POOL_GROUP)
        lhs = jnp.concatenate(pooled[2 * pair:2 * pair + 2], axis=1)
        m = jnp.dot(lhs, wpool_ref[pair], preferred_element_type=jnp.float32)
        mixed.append((m * pscale_ref[:, sl]).astype(jnp.bfloat16))
    pool_out = jnp.concatenate(mixed, axis=1)

    y = y + jnp.dot(pool_out, wout_ref[:POOL_WIDTH, :], preferred_element_type=jnp.float32)
    o_ref[...] = x_ref[...] + y


def _mixers(x, u, q, k, v, w_pool_pairs, pool_scale, rpb_rows, w_out_bf16):
    B, T, D = x.shape
    n_heads, n_dr, _ = rpb_rows.shape
    n_t = T // TOKEN_TILE
    halo_per_tile = TOKEN_TILE // HALO
    n_halo_blocks = T // HALO
    tile = lambda width: pl.BlockSpec((None, TOKEN_TILE, width), lambda b, t: (b, t, 0))
    whole_seq = pl.BlockSpec((None, T, NA_WIDTH), lambda b, t: (b, 0, 0))
    const = lambda a: pl.BlockSpec(a.shape, lambda b, t: (0,) * a.ndim)
    return pl.pallas_call(
        _mixers_kernel,
        grid=(B, n_t),
        in_specs=[
            tile(D),
            tile(POOL_WIDTH),
            pl.BlockSpec((None, HALO, POOL_WIDTH),
                         lambda b, t: (b, jnp.maximum(t * halo_per_tile - 1, 0), 0)),
            pl.BlockSpec((None, HALO, POOL_WIDTH),
                         lambda b, t: (b, jnp.minimum((t + 1) * halo_per_tile, n_halo_blocks - 1), 0)),
            tile(NA_WIDTH),
            whole_seq,
            whole_seq,
            const(w_pool_pairs), const(pool_scale), const(rpb_rows), const(w_out_bf16),
        ],
        out_specs=tile(D),
        out_shape=jax.ShapeDtypeStruct((B, T, D), jnp.float32),
        scratch_shapes=[
            pltpu.VMEM((TOKEN_TILE, NA_WIDTH), jnp.bfloat16),
            pltpu.VMEM((n_dr - 1, n_heads * GRID_W, 2 * GRID_W), jnp.float32),
        ],
        compiler_params=pltpu.CompilerParams(
            dimension_semantics=("arbitrary", "arbitrary"),
            vmem_limit_bytes=V7X_VMEM_LIMIT_BYTES),
        name="mixers",
    )(x, u, u, u, q, k, v, w_pool_pairs, pool_scale, rpb_rows, w_out_bf16)


def _mlp_kernel(x_ref, g_ref, wup_ref, wdown_ref, gf_ref, o_ref):
    x = x_ref[...]
    h = _rmsnorm_f32(x, g_ref[...]).astype(jnp.bfloat16)
    acc = x
    for c in range(D_FF // FF_CHUNK):
        cs = slice(c * FF_CHUNK, (c + 1) * FF_CHUNK)
        up = jnp.dot(h, wup_ref[:, cs], preferred_element_type=jnp.float32)
        act = jnp.square(jnp.maximum(up, 0.0)).astype(jnp.bfloat16)
        acc = acc + jnp.dot(act, wdown_ref[cs, :], preferred_element_type=jnp.float32)
    o_ref[...] = _rmsnorm_f32(acc, gf_ref[...])


def _mlp(x, g, w_up_bf16, w_down_bf16, g_final):
    B, T, D = x.shape
    n_t = T // DENSE_TILE
    tile = pl.BlockSpec((None, DENSE_TILE, D), lambda b, t: (b, t, 0))
    const = lambda a: pl.BlockSpec(a.shape, lambda b, t: (0,) * a.ndim,
                                   pipeline_mode=pl.Buffered(1))
    return pl.pallas_call(
        _mlp_kernel,
        grid=(B, n_t),
        in_specs=[tile, const(g), const(w_up_bf16), const(w_down_bf16), const(g_final)],
        out_specs=tile,
        out_shape=jax.ShapeDtypeStruct((B, T, D), jnp.float32),
        compiler_params=pltpu.CompilerParams(
            dimension_semantics=("arbitrary", "arbitrary"),
            vmem_limit_bytes=V7X_VMEM_LIMIT_BYTES),
        name="mlp",
    )(x, g, w_up_bf16, w_down_bf16, g_final)


def _pool_weight_pairs(w_pool_bf16):
    n_groups, cg, _ = w_pool_bf16.shape
    zero = jnp.zeros((cg, cg), w_pool_bf16.dtype)
    pairs = [jnp.block([[w_pool_bf16[2 * p], zero], [zero, w_pool_bf16[2 * p + 1]]])
             for p in range(n_groups // 2)]
    return jnp.stack(pairs)


def _rpb_rows(rpb):
    n_dc = rpb.shape[-1]
    padded = jnp.pad(rpb, ((0, 0), (0, 0), (0, 2 * GRID_W - n_dc)))
    return jnp.roll(padded, -(WIN_COLS - 1), axis=-1)


def kernel(x, norm_mix_g, w_in, w_pool, pool_scale, rpb, w_out, norm_mlp_g, w_up, w_down, final_g):
    bf16 = jnp.bfloat16
    assert w_in.shape[0] == 1, "the final rmsnorm is fused into the (single) layer's MLP call"
    assert MAX_HALF_WINDOW < HALO
    u, q, k, v = _in_proj(x, norm_mix_g[0][None, :], w_in[0].astype(bf16))
    x = _mixers(x, u, q, k, v, _pool_weight_pairs(w_pool[0].astype(bf16)), pool_scale[0][None, :],
                _rpb_rows(rpb[0]), w_out[0].astype(bf16))
    return _mlp(x, norm_mlp_g[0][None, :], w_up[0].astype(bf16), w_down[0].astype(bf16),
                final_g[None, :])
```

```python
import jax
import jax.numpy as jnp
import numpy as np
from jax import lax
from jax.experimental import pallas as pl
from jax.experimental.pallas import tpu as pltpu

D_MODEL = 1024
POOL_WIDTH = 512
POOL_WINDOWS = (2, 4, 8, 16)
POOL_GROUP = 128
NA_WIDTH = 512
HEAD_DIM = 64
N_HEADS = 8
GRID_W = 64
WIN_ROWS = 8
WIN_COLS = 16
D_FF = 4096
EPS = 1e-6

HEADS_PER_GROUP = 4
GROUP_WIDTH = HEADS_PER_GROUP * HEAD_DIM
N_GROUPS = N_HEADS // HEADS_PER_GROUP
KEYS_PER_ROW_WINDOW = WIN_ROWS * GRID_W
MASK_VALUE = -1e30
LOG2_E = float(np.log2(np.e))
QUERY_SCALE = HEAD_DIM ** -0.5 * LOG2_E

TOKEN_TILE = 512
ROWS_PER_TILE = TOKEN_TILE // GRID_W
DENSE_TILE = 1024
HALO = 16
SUBLANES = 8
FF_CHUNK = 1024

V7X_VMEM_LIMIT_BYTES = 56 * 1024 * 1024


def _inv_rms(x):
    return lax.rsqrt(jnp.mean(jnp.square(x), axis=-1, keepdims=True) + EPS)


def _in_proj_kernel(x_ref, g_ref, w_ref, *refs):
    u_ref, q_ref, k_ref, v_ref = refs[-4:]
    later_weights = refs[:-4]
    n_later = len(later_weights) // 2
    for src, dst in zip(later_weights[:n_later], later_weights[n_later:]):
        dst[...] = src[...].astype(jnp.bfloat16)

    x = x_ref[...]
    inv_rms = _inv_rms(x)
    proj = jnp.dot((x * g_ref[...]).astype(jnp.bfloat16), w_ref[...],
                   preferred_element_type=jnp.float32)
    u_ref[...] = (proj[:, :POOL_WIDTH] * inv_rms).astype(jnp.bfloat16)
    q = proj[:, POOL_WIDTH:POOL_WIDTH + NA_WIDTH] * (inv_rms * QUERY_SCALE)
    q_ref[...] = q.astype(jnp.bfloat16)
    k = proj[:, POOL_WIDTH + NA_WIDTH:POOL_WIDTH + 2 * NA_WIDTH] * inv_rms
    k_ref[...] = k.astype(jnp.bfloat16)
    v_ref[...] = (proj[:, POOL_WIDTH + 2 * NA_WIDTH:] * inv_rms).astype(jnp.bfloat16)


def _in_proj(x, g, w_in_bf16, later_weights):
    B, T, D = x.shape
    n_t = T // DENSE_TILE
    n_steps = B * n_t
    tile = lambda width: pl.BlockSpec((None, DENSE_TILE, width), lambda b, t: (b, t, 0))
    out = jax.ShapeDtypeStruct((B, T, POOL_WIDTH), jnp.bfloat16)

    def slab(w):
        rows, cols = w.shape
        assert rows % (n_steps * HALO) == 0
        return pl.BlockSpec((rows // n_steps, cols), lambda b, t: (b * n_t + t, 0))

    results = pl.pallas_call(
        _in_proj_kernel,
        grid=(B, n_t),
        in_specs=[
            tile(D),
            pl.BlockSpec((1, D), lambda b, t: (0, 0)),
            pl.BlockSpec(w_in_bf16.shape, lambda b, t: (0, 0)),
        ] + [slab(w) for w in later_weights],
        out_specs=[slab(w) for w in later_weights] + [tile(POOL_WIDTH)] * 4,
        out_shape=[jax.ShapeDtypeStruct(w.shape, jnp.bfloat16) for w in later_weights] + [out] * 4,
        compiler_params=pltpu.CompilerParams(
            dimension_semantics=("arbitrary", "arbitrary"),
            vmem_limit_bytes=V7X_VMEM_LIMIT_BYTES),
        name="in_proj",
    )(x, g, w_in_bf16, *later_weights)
    return results[-4:], results[:-4]


def _build_bias_table(rpb_ref, bias_ref):
    n_heads, n_dr, lanes = rpb_ref.shape
    assert lanes == 2 * GRID_W
    qc = lax.broadcasted_iota(jnp.int32, (GRID_W, lanes), 0)
    lane = lax.broadcasted_iota(jnp.int32, (GRID_W, lanes), 1)
    kc = lane % GRID_W
    c0 = jnp.clip(qc - WIN_COLS // 2, 0, GRID_W - WIN_COLS)
    in_window = (kc >= c0) & (kc < c0 + WIN_COLS)
    first_row = lane < GRID_W
    for h in range(n_heads):
        skew = []
        for dr in range(n_dr):
            row = jnp.broadcast_to(rpb_ref[h, dr:dr + 1, :], (GRID_W, lanes))
            skew.append(pltpu.roll(row, 0, axis=1, stride=1, stride_axis=0))
        for p in range(n_dr - 1):
            pair = jnp.where(first_row, skew[p], pltpu.roll(skew[p + 1], GRID_W, axis=1))
            bias_ref[p, h * GRID_W:(h + 1) * GRID_W, :] = jnp.where(in_window, pair * LOG2_E, MASK_VALUE)


def _pooled_minus_token(x, w, first_token, seq_len):
    halo_len = x.shape[0]
    n = halo_len - 2 * HALO
    assert w // 2 <= SUBLANES <= HALO
    t_top = first_token + lax.broadcasted_iota(jnp.int32, (SUBLANES, 1), 0)
    t_bot = t_top + (n - SUBLANES)

    def shifted(a, s):
        return pltpu.roll(a, (-s) % halo_len, axis=0)

    def inv_count(t):
        lo = jnp.maximum(t - w // 2, 0)
        hi = jnp.minimum(t - w // 2 + w, seq_len)
        return 1.0 / (hi - lo).astype(jnp.float32)

    run = x
    for k in range(w.bit_length() - 2):
        run = run + shifted(run, 2 ** k)
    win_sum = (shifted(run, -(w // 2)) + run)[HALO:HALO + n]
    mean = jnp.concatenate([
        win_sum[:SUBLANES] * inv_count(t_top),
        win_sum[SUBLANES:n - SUBLANES] * (1.0 / w),
        win_sum[n - SUBLANES:] * inv_count(t_bot)], axis=0)
    return mean - x[HALO:HALO + n]


def _mixers_kernel(x_ref, u_ref, u_prev_ref, u_next_ref, q_ref, k_ref, v_ref,
                   wpool_ref, pscale_ref, rpb_ref, wout_ref,
                   o_ref, attn_ref, bias_ref):
    tb = pl.program_id(1)
    n_tb = pl.num_programs(1)
    seq_len = n_tb * TOKEN_TILE

    @pl.when((pl.program_id(0) == 0) & (tb == 0))
    def _():
        _build_bias_table(rpb_ref, bias_ref)

    n_rows = n_tb * ROWS_PER_TILE
    kr = min(WIN_ROWS, n_rows)
    lane_head = lax.broadcasted_iota(jnp.int32, (GRID_W, GROUP_WIDTH), 1) // HEAD_DIM
    head_sel = [(lane_head == h).astype(jnp.float32).astype(jnp.bfloat16) > 0
                for h in range(HEADS_PER_GROUP)]

    def attend_row(rl):
        r = tb * ROWS_PER_TILE + rl
        r0 = jnp.clip(r - kr // 2, 0, n_rows - kr)
        dr0 = r0 - r + (WIN_ROWS - 1)
        q_rows = slice(rl * GRID_W, (rl + 1) * GRID_W)
        k_start = pl.multiple_of(r0 * GRID_W, GRID_W)
        for g in range(N_GROUPS):
            gs = slice(g * GROUP_WIDTH, (g + 1) * GROUP_WIDTH)
            q_row = q_ref[q_rows, gs]
            q_bd = jnp.concatenate(
                [jnp.where(head_sel[h], q_row, jnp.zeros_like(q_row))
                 for h in range(HEADS_PER_GROUP)], axis=0)
            k_rows = k_ref[pl.ds(k_start, KEYS_PER_ROW_WINDOW), gs]
            v_rows = v_ref[pl.ds(k_start, KEYS_PER_ROW_WINDOW), gs]
            s = lax.dot_general(q_bd, k_rows, (((1,), (1,)), ((), ())),
                                preferred_element_type=jnp.float32)
            bias = jnp.concatenate(
                [bias_ref[dr0 + 2 * j, gs, :] for j in range(WIN_ROWS // 2)], axis=1)
            s = s + bias
            m = jnp.max(s, axis=-1, keepdims=True)
            p = jnp.exp2(s - m)
            l = jnp.sum(p, axis=-1, keepdims=True)
            o = jnp.dot(p.astype(jnp.bfloat16), v_rows,
                        preferred_element_type=jnp.float32)
            o = o / l
            og = jnp.zeros((GRID_W, GROUP_WIDTH), jnp.float32)
            for h in range(HEADS_PER_GROUP):
                og = jnp.where(lane_head == h, o[h * GRID_W:(h + 1) * GRID_W, :], og)
            attn_ref[q_rows, gs] = og.astype(jnp.bfloat16)

    for rl in range(ROWS_PER_TILE):
        attend_row(rl)

    y = jnp.dot(attn_ref[...], wout_ref[POOL_WIDTH:, :], preferred_element_type=jnp.float32)

    u_prev = jnp.where(tb > 0, u_prev_ref[...], jnp.zeros_like(u_prev_ref))
    u_next = jnp.where(tb < n_tb - 1, u_next_ref[...], jnp.zeros_like(u_next_ref))
    u_halo = jnp.concatenate([u_prev, u_ref[...], u_next], axis=0).astype(jnp.float32)
    pooled = [
        _pooled_minus_token(u_halo[:, gi * POOL_GROUP:(gi + 1) * POOL_GROUP], w,
                            tb * TOKEN_TILE, seq_len).astype(jnp.bfloat16)
        for gi, w in enumerate(POOL_WINDOWS)]
    mixed = []
    for pair in range(len(POOL_WINDOWS) // 2):
        sl = slice(pair * 2 * POOL_GROUP, (pair + 1) * 2 * POOL_GROUP)
        lhs = jnp.concatenate(pooled[2 * pair:2 * pair + 2], axis=1)
        m = jnp.dot(lhs, wpool_ref[pair], preferred_element_type=jnp.float32)
        mixed.append((m * pscale_ref[:, sl]).astype(jnp.bfloat16))
    pool_out = jnp.concatenate(mixed, axis=1)

    y = y + jnp.dot(pool_out, wout_ref[:POOL_WIDTH, :], preferred_element_type=jnp.float32)
    o_ref[...] = x_ref[...] + y


def _mixers(x, u, q, k, v, w_pool_pairs, pool_scale, rpb_rows, w_out_bf16):
    B, T, D = x.shape
    n_heads, n_dr, _ = rpb_rows.shape
    n_t = T // TOKEN_TILE
    halo_per_tile = TOKEN_TILE // HALO
    n_halo_blocks = T // HALO
    tile = lambda width: pl.BlockSpec((None, TOKEN_TILE, width), lambda b, t: (b, t, 0))
    whole_seq = pl.BlockSpec((None, T, NA_WIDTH), lambda b, t: (b, 0, 0))
    const = lambda a: pl.BlockSpec(a.shape, lambda b, t: (0,) * a.ndim)
    return pl.pallas_call(
        _mixers_kernel,
        grid=(B, n_t),
        in_specs=[
            tile(D),
            tile(POOL_WIDTH),
            pl.BlockSpec((None, HALO, POOL_WIDTH),
                         lambda b, t: (b, jnp.maximum(t * halo_per_tile - 1, 0), 0)),
            pl.BlockSpec((None, HALO, POOL_WIDTH),
                         lambda b, t: (b, jnp.minimum((t + 1) * halo_per_tile, n_halo_blocks - 1), 0)),
            tile(NA_WIDTH),
            whole_seq,
            whole_seq,
            const(w_pool_pairs), const(pool_scale), const(rpb_rows), const(w_out_bf16),
        ],
        out_specs=tile(D),
        out_shape=jax.ShapeDtypeStruct((B, T, D), jnp.float32),
        scratch_shapes=[
            pltpu.VMEM((TOKEN_TILE, NA_WIDTH), jnp.bfloat16),
            pltpu.VMEM((n_dr - 1, n_heads * GRID_W, 2 * GRID_W), jnp.float32),
        ],
        compiler_params=pltpu.CompilerParams(
            dimension_semantics=("arbitrary", "arbitrary"),
            vmem_limit_bytes=V7X_VMEM_LIMIT_BYTES),
        name="mixers",
    )(x, u, u, u, q, k, v, w_pool_pairs, pool_scale, rpb_rows, w_out_bf16)


def _mlp_kernel(x_ref, g_ref, wup_ref, wdown_ref, gf_ref, o_ref):
    x = x_ref[...]
    inv_rms = _inv_rms(x)
    h = (x * g_ref[...]).astype(jnp.bfloat16)
    acc = x
    for c in range(D_FF // FF_CHUNK):
        cs = slice(c * FF_CHUNK, (c + 1) * FF_CHUNK)
        up = jnp.dot(h, wup_ref[:, cs], preferred_element_type=jnp.float32) * inv_rms
        act = jnp.square(jnp.maximum(up, 0.0)).astype(jnp.bfloat16)
        acc = acc + jnp.dot(act, wdown_ref[cs, :], preferred_element_type=jnp.float32)
    o_ref[...] = acc * _inv_rms(acc) * gf_ref[...]


def _mlp(x, g, w_up_bf16, w_down_bf16, g_final):
    B, T, D = x.shape
    n_t = T // DENSE_TILE
    tile = pl.BlockSpec((None, DENSE_TILE, D), lambda b, t: (b, t, 0))
    const = lambda a: pl.BlockSpec(a.shape, lambda b, t: (0,) * a.ndim,
                                   pipeline_mode=pl.Buffered(1))
    return pl.pallas_call(
        _mlp_kernel,
        grid=(B, n_t),
        in_specs=[tile, const(g), const(w_up_bf16), const(w_down_bf16), const(g_final)],
        out_specs=tile,
        out_shape=jax.ShapeDtypeStruct((B, T, D), jnp.float32),
        compiler_params=pltpu.CompilerParams(
            dimension_semantics=("arbitrary", "arbitrary"),
            vmem_limit_bytes=V7X_VMEM_LIMIT_BYTES),
        name="mlp",
    )(x, g, w_up_bf16, w_down_bf16, g_final)


def _pool_weight_pairs(w_pool_bf16):
    n_groups, cg, _ = w_pool_bf16.shape
    zero = jnp.zeros((cg, cg), w_pool_bf16.dtype)
    pairs = [jnp.block([[w_pool_bf16[2 * p], zero], [zero, w_pool_bf16[2 * p + 1]]])
             for p in range(n_groups // 2)]
    return jnp.stack(pairs)


def _rpb_rows(rpb):
    n_dc = rpb.shape[-1]
    padded = jnp.pad(rpb, ((0, 0), (0, 0), (0, 2 * GRID_W - n_dc)))
    return jnp.roll(padded, -(WIN_COLS - 1), axis=-1)


def kernel(x, norm_mix_g, w_in, w_pool, pool_scale, rpb, w_out, norm_mlp_g, w_up, w_down, final_g):
    bf16 = jnp.bfloat16
    assert w_in.shape[0] == 1, "the final rmsnorm is fused into the (single) layer's MLP call"
    (u, q, k, v), (w_out_bf16, w_up_bf16, w_down_bf16) = _in_proj(
        x, norm_mix_g[0][None, :], w_in[0].astype(bf16), (w_out[0], w_up[0], w_down[0]))
    x = _mixers(x, u, q, k, v, _pool_weight_pairs(w_pool[0].astype(bf16)), pool_scale[0][None, :],
                _rpb_rows(rpb[0]), w_out_bf16)
    return _mlp(x, norm_mlp_g[0][None, :], w_up_bf16, w_down_bf16, final_g[None, :])
```

```python
import jax
import jax.numpy as jnp
import numpy as np
from jax import lax
from jax.experimental import pallas as pl
from jax.experimental.pallas import tpu as pltpu

D_MODEL = 1024
POOL_WIDTH = 512
POOL_WINDOWS = (2, 4, 8, 16)
POOL_GROUP = 128
NA_WIDTH = 512
HEAD_DIM = 64
N_HEADS = 8
GRID_W = 64
WIN_ROWS = 8
WIN_COLS = 16
D_FF = 4096
EPS = 1e-6

HEADS_PER_GROUP = 4
GROUP_WIDTH = HEADS_PER_GROUP * HEAD_DIM
N_GROUPS = N_HEADS // HEADS_PER_GROUP
KEYS_PER_ROW_WINDOW = WIN_ROWS * GRID_W
MASK_VALUE = -1e30
LOG2_E = float(np.log2(np.e))
QUERY_SCALE = HEAD_DIM ** -0.5 * LOG2_E

TOKEN_TILE = 512
ROWS_PER_TILE = TOKEN_TILE // GRID_W
DENSE_TILE = 1024
HALO = 16
SUBLANES = 8
FF_CHUNK = 1024

V7X_VMEM_LIMIT_BYTES = 56 * 1024 * 1024


def _inv_rms(x):
    return lax.rsqrt(jnp.mean(jnp.square(x), axis=-1, keepdims=True) + EPS)


def _in_proj_kernel(x_ref, g_ref, w_ref, *refs):
    u_ref, q_ref, k_ref, v_ref, w_bf16_ref = refs[-5:]

    @pl.when((pl.program_id(0) == 0) & (pl.program_id(1) == 0))
    def _():
        w_bf16_ref[...] = w_ref[...].astype(jnp.bfloat16)

    later_weights = refs[:-5]
    n_later = len(later_weights) // 2
    for src, dst in zip(later_weights[:n_later], later_weights[n_later:]):
        dst[...] = src[...].astype(jnp.bfloat16)

    x = x_ref[...]
    inv_rms = _inv_rms(x)
    proj = jnp.dot((x * g_ref[...]).astype(jnp.bfloat16), w_bf16_ref[...],
                   preferred_element_type=jnp.float32)
    u_ref[...] = (proj[:, :POOL_WIDTH] * inv_rms).astype(jnp.bfloat16)
    q = proj[:, POOL_WIDTH:POOL_WIDTH + NA_WIDTH] * (inv_rms * QUERY_SCALE)
    q_ref[...] = q.astype(jnp.bfloat16)
    k = proj[:, POOL_WIDTH + NA_WIDTH:POOL_WIDTH + 2 * NA_WIDTH] * inv_rms
    k_ref[...] = k.astype(jnp.bfloat16)
    v_ref[...] = (proj[:, POOL_WIDTH + 2 * NA_WIDTH:] * inv_rms).astype(jnp.bfloat16)


def _in_proj(x, g, w_in, later_weights):
    B, T, D = x.shape
    n_t = T // DENSE_TILE
    n_steps = B * n_t
    tile = lambda width: pl.BlockSpec((None, DENSE_TILE, width), lambda b, t: (b, t, 0))
    out = jax.ShapeDtypeStruct((B, T, POOL_WIDTH), jnp.bfloat16)

    def slab(w):
        rows, cols = w.shape
        assert rows % (n_steps * HALO) == 0
        return pl.BlockSpec((rows // n_steps, cols), lambda b, t: (b * n_t + t, 0))

    results = pl.pallas_call(
        _in_proj_kernel,
        grid=(B, n_t),
        in_specs=[
            tile(D),
            pl.BlockSpec((1, D), lambda b, t: (0, 0)),
            pl.BlockSpec(w_in.shape, lambda b, t: (0, 0), pipeline_mode=pl.Buffered(1)),
        ] + [slab(w) for w in later_weights],
        out_specs=[slab(w) for w in later_weights] + [tile(POOL_WIDTH)] * 4,
        out_shape=[jax.ShapeDtypeStruct(w.shape, jnp.bfloat16) for w in later_weights] + [out] * 4,
        scratch_shapes=[pltpu.VMEM(w_in.shape, jnp.bfloat16)],
        compiler_params=pltpu.CompilerParams(
            dimension_semantics=("arbitrary", "arbitrary"),
            vmem_limit_bytes=V7X_VMEM_LIMIT_BYTES),
        name="in_proj",
    )(x, g, w_in, *later_weights)
    return results[-4:], results[:-4]


def _build_bias_table(rpb_ref, bias_ref):
    n_heads, n_dr, lanes = rpb_ref.shape
    assert lanes == 2 * GRID_W
    qc = lax.broadcasted_iota(jnp.int32, (GRID_W, lanes), 0)
    lane = lax.broadcasted_iota(jnp.int32, (GRID_W, lanes), 1)
    kc = lane % GRID_W
    c0 = jnp.clip(qc - WIN_COLS // 2, 0, GRID_W - WIN_COLS)
    in_window = (kc >= c0) & (kc < c0 + WIN_COLS)
    first_row = lane < GRID_W
    for h in range(n_heads):
        skew = []
        for dr in range(n_dr):
            row = jnp.broadcast_to(rpb_ref[h, dr:dr + 1, :], (GRID_W, lanes))
            skew.append(pltpu.roll(row, 0, axis=1, stride=1, stride_axis=0))
        for p in range(n_dr - 1):
            pair = jnp.where(first_row, skew[p], pltpu.roll(skew[p + 1], GRID_W, axis=1))
            bias_ref[p, h * GRID_W:(h + 1) * GRID_W, :] = jnp.where(in_window, pair * LOG2_E, MASK_VALUE)


def _pooled_minus_token(x, w, first_token, seq_len):
    halo_len = x.shape[0]
    n = halo_len - 2 * HALO
    assert w // 2 <= SUBLANES <= HALO
    t_top = first_token + lax.broadcasted_iota(jnp.int32, (SUBLANES, 1), 0)
    t_bot = t_top + (n - SUBLANES)

    def shifted(a, s):
        return pltpu.roll(a, (-s) % halo_len, axis=0)

    def inv_count(t):
        lo = jnp.maximum(t - w // 2, 0)
        hi = jnp.minimum(t - w // 2 + w, seq_len)
        return 1.0 / (hi - lo).astype(jnp.float32)

    run = x
    for k in range(w.bit_length() - 2):
        run = run + shifted(run, 2 ** k)
    win_sum = (shifted(run, -(w // 2)) + run)[HALO:HALO + n]
    mean = jnp.concatenate([
        win_sum[:SUBLANES] * inv_count(t_top),
        win_sum[SUBLANES:n - SUBLANES] * (1.0 / w),
        win_sum[n - SUBLANES:] * inv_count(t_bot)], axis=0)
    return mean - x[HALO:HALO + n]


def _mixers_kernel(x_ref, u_ref, u_prev_ref, u_next_ref, q_ref, k_ref, v_ref,
                   wpool_ref, pscale_ref, rpb_ref, wout_ref,
                   o_ref, attn_ref, bias_ref):
    tb = pl.program_id(1)
    n_tb = pl.num_programs(1)
    seq_len = n_tb * TOKEN_TILE

    @pl.when((pl.program_id(0) == 0) & (tb == 0))
    def _():
        _build_bias_table(rpb_ref, bias_ref)

    n_rows = n_tb * ROWS_PER_TILE
    kr = min(WIN_ROWS, n_rows)
    lane_head = lax.broadcasted_iota(jnp.int32, (GRID_W, GROUP_WIDTH), 1) // HEAD_DIM
    head_sel = [(lane_head == h).astype(jnp.float32).astype(jnp.bfloat16) > 0
                for h in range(HEADS_PER_GROUP)]

    def attend_row(rl):
        r = tb * ROWS_PER_TILE + rl
        r0 = jnp.clip(r - kr // 2, 0, n_rows - kr)
        dr0 = r0 - r + (WIN_ROWS - 1)
        q_rows = slice(rl * GRID_W, (rl + 1) * GRID_W)
        k_start = pl.multiple_of(r0 * GRID_W, GRID_W)
        for g in range(N_GROUPS):
            gs = slice(g * GROUP_WIDTH, (g + 1) * GROUP_WIDTH)
            q_row = q_ref[q_rows, gs]
            q_bd = jnp.concatenate(
                [jnp.where(head_sel[h], q_row, jnp.zeros_like(q_row))
                 for h in range(HEADS_PER_GROUP)], axis=0)
            k_rows = k_ref[pl.ds(k_start, KEYS_PER_ROW_WINDOW), gs]
            v_rows = v_ref[pl.ds(k_start, KEYS_PER_ROW_WINDOW), gs]
            s = lax.dot_general(q_bd, k_rows, (((1,), (1,)), ((), ())),
                                preferred_element_type=jnp.float32)
            bias = jnp.concatenate(
                [bias_ref[dr0 + 2 * j, gs, :] for j in range(WIN_ROWS // 2)], axis=1)
            s = s + bias
            m = jnp.max(s, axis=-1, keepdims=True)
            p = jnp.exp2(s - m)
            l = jnp.sum(p, axis=-1, keepdims=True)
            o = jnp.dot(p.astype(jnp.bfloat16), v_rows,
                        preferred_element_type=jnp.float32)
            o = o / l
            og = jnp.zeros((GRID_W, GROUP_WIDTH), jnp.float32)
            for h in range(HEADS_PER_GROUP):
                og = jnp.where(lane_head == h, o[h * GRID_W:(h + 1) * GRID_W, :], og)
            attn_ref[q_rows, gs] = og.astype(jnp.bfloat16)

    def pool_pair(pair):
        u_prev = jnp.where(tb > 0, u_prev_ref[...], jnp.zeros_like(u_prev_ref))
        u_next = jnp.where(tb < n_tb - 1, u_next_ref[...], jnp.zeros_like(u_next_ref))
        sl = slice(pair * 2 * POOL_GROUP, (pair + 1) * 2 * POOL_GROUP)
        u_halo = jnp.concatenate([u_prev[:, sl], u_ref[:, sl], u_next[:, sl]], axis=0).astype(jnp.float32)
        pooled = [
            _pooled_minus_token(u_halo[:, i * POOL_GROUP:(i + 1) * POOL_GROUP], POOL_WINDOWS[2 * pair + i],
                                tb * TOKEN_TILE, seq_len).astype(jnp.bfloat16)
            for i in range(2)]
        m = jnp.dot(jnp.concatenate(pooled, axis=1), wpool_ref[pair], preferred_element_type=jnp.float32)
        return (m * pscale_ref[:, sl]).astype(jnp.bfloat16)

    pool_out = jnp.concatenate([pool_pair(0), pool_pair(1)], axis=1)

    for rl in range(ROWS_PER_TILE):
        attend_row(rl)

    y = jnp.dot(jnp.concatenate([pool_out, attn_ref[...]], axis=1), wout_ref[...],
                preferred_element_type=jnp.float32)
    o_ref[...] = x_ref[...] + y


def _mixers(x, u, q, k, v, w_pool_pairs, pool_scale, rpb_rows, w_out_bf16):
    B, T, D = x.shape
    n_heads, n_dr, _ = rpb_rows.shape
    n_t = T // TOKEN_TILE
    halo_per_tile = TOKEN_TILE // HALO
    n_halo_blocks = T // HALO
    tile = lambda width: pl.BlockSpec((None, TOKEN_TILE, width), lambda b, t: (b, t, 0))
    whole_seq = pl.BlockSpec((None, T, NA_WIDTH), lambda b, t: (b, 0, 0))
    const = lambda a: pl.BlockSpec(a.shape, lambda b, t: (0,) * a.ndim)
    return pl.pallas_call(
        _mixers_kernel,
        grid=(B, n_t),
        in_specs=[
            tile(D),
            tile(POOL_WIDTH),
            pl.BlockSpec((None, HALO, POOL_WIDTH),
                         lambda b, t: (b, jnp.maximum(t * halo_per_tile - 1, 0), 0)),
            pl.BlockSpec((None, HALO, POOL_WIDTH),
                         lambda b, t: (b, jnp.minimum((t + 1) * halo_per_tile, n_halo_blocks - 1), 0)),
            tile(NA_WIDTH),
            whole_seq,
            whole_seq,
            const(w_pool_pairs), const(pool_scale), const(rpb_rows), const(w_out_bf16),
        ],
        out_specs=tile(D),
        out_shape=jax.ShapeDtypeStruct((B, T, D), jnp.float32),
        scratch_shapes=[
            pltpu.VMEM((TOKEN_TILE, NA_WIDTH), jnp.bfloat16),
            pltpu.VMEM((n_dr - 1, n_heads * GRID_W, 2 * GRID_W), jnp.float32),
        ],
        compiler_params=pltpu.CompilerParams(
            dimension_semantics=("arbitrary", "arbitrary"),
            vmem_limit_bytes=V7X_VMEM_LIMIT_BYTES),
        name="mixers",
    )(x, u, u, u, q, k, v, w_pool_pairs, pool_scale, rpb_rows, w_out_bf16)


def _mlp_kernel(x_ref, g_ref, wup_ref, wdown_ref, gf_ref, o_ref):
    x = x_ref[...]
    inv_rms = _inv_rms(x)
    h = (x * g_ref[...]).astype(jnp.bfloat16)
    acc = x
    for c in range(D_FF // FF_CHUNK):
        cs = slice(c * FF_CHUNK, (c + 1) * FF_CHUNK)
        up = jnp.dot(h, wup_ref[:, cs], preferred_element_type=jnp.float32) * inv_rms
        act = jnp.square(jnp.maximum(up, 0.0)).astype(jnp.bfloat16)
        acc = acc + jnp.dot(act, wdown_ref[cs, :], preferred_element_type=jnp.float32)
    o_ref[...] = acc * _inv_rms(acc) * gf_ref[...]


def _mlp(x, g, w_up_bf16, w_down_bf16, g_final):
    B, T, D = x.shape
    n_t = T // DENSE_TILE
    tile = pl.BlockSpec((None, DENSE_TILE, D), lambda b, t: (b, t, 0))
    const = lambda a: pl.BlockSpec(a.shape, lambda b, t: (0,) * a.ndim,
                                   pipeline_mode=pl.Buffered(1))
    return pl.pallas_call(
        _mlp_kernel,
        grid=(B, n_t),
        in_specs=[tile, const(g), const(w_up_bf16), const(w_down_bf16), const(g_final)],
        out_specs=tile,
        out_shape=jax.ShapeDtypeStruct((B, T, D), jnp.float32),
        compiler_params=pltpu.CompilerParams(
            dimension_semantics=("arbitrary", "arbitrary"),
            vmem_limit_bytes=V7X_VMEM_LIMIT_BYTES),
        name="mlp",
    )(x, g, w_up_bf16, w_down_bf16, g_final)


def _pool_weight_pairs(w_pool_bf16):
    n_groups, cg, _ = w_pool_bf16.shape
    zero = jnp.zeros((cg, cg), w_pool_bf16.dtype)
    pairs = [jnp.block([[w_pool_bf16[2 * p], zero], [zero, w_pool_bf16[2 * p + 1]]])
             for p in range(n_groups // 2)]
    return jnp.stack(pairs)


def _rpb_rows(rpb):
    n_dc = rpb.shape[-1]
    padded = jnp.pad(rpb, ((0, 0), (0, 0), (0, 2 * GRID_W - n_dc)))
    return jnp.roll(padded, -(WIN_COLS - 1), axis=-1)


def kernel(x, norm_mix_g, w_in, w_pool, pool_scale, rpb, w_out, norm_mlp_g, w_up, w_down, final_g):
    bf16 = jnp.bfloat16
    assert w_in.shape[0] == 1, "the final rmsnorm is fused into the (single) layer's MLP call"
    (u, q, k, v), (w_out_bf16, w_up_bf16, w_down_bf16) = _in_proj(
        x, norm_mix_g[0][None, :], w_in[0], (w_out[0], w_up[0], w_down[0]))
    x = _mixers(x, u, q, k, v, _pool_weight_pairs(w_pool[0].astype(bf16)), pool_scale[0][None, :],
                _rpb_rows(rpb[0]), w_out_bf16)
    return _mlp(x, norm_mlp_g[0][None, :], w_up_bf16, w_down_bf16, final_g[None, :])
```

```python
import jax
import jax.numpy as jnp
import numpy as np
from jax import lax
from jax.experimental import pallas as pl
from jax.experimental.pallas import tpu as pltpu

POOL_WIDTH = 512
POOL_WINDOWS = (2, 4, 8, 16)
POOL_GROUP = 128
NA_WIDTH = 512
HEAD_DIM = 64
N_HEADS = 8
GRID_W = 64
WIN_ROWS = 8
WIN_COLS = 16
D_FF = 4096
EPS = 1e-6

HEADS_PER_GROUP = 4
GROUP_WIDTH = HEADS_PER_GROUP * HEAD_DIM
N_GROUPS = N_HEADS // HEADS_PER_GROUP
KEYS_PER_ROW_WINDOW = WIN_ROWS * GRID_W
MASK_VALUE = -1e30
LOG2_E = float(np.log2(np.e))
QUERY_SCALE = HEAD_DIM ** -0.5 * LOG2_E

TOKEN_TILE = 512
ROWS_PER_TILE = TOKEN_TILE // GRID_W
DENSE_TILE = 1024
SUBLANES = 8
BF16_SUBLANES = 16
HALO = BF16_SUBLANES
FF_CHUNK = 1024
ROW_BLOCKS = 4

V7X_VMEM_LIMIT_BYTES = 56 * 1024 * 1024


def _inv_rms(x):
    return lax.rsqrt(jnp.mean(jnp.square(x), axis=-1, keepdims=True) + EPS)


def _in_proj_kernel(x_ref, g_ref, w_ref, *refs):
    u_ref, q_ref, k_ref, v_ref, w_bf16_ref = refs[-5:]

    @pl.when((pl.program_id(0) == 0) & (pl.program_id(1) == 0))
    def _():
        w_bf16_ref[...] = w_ref[...].astype(jnp.bfloat16)

    later_weights = refs[:-5]
    n_later = len(later_weights) // 2
    for src, dst in zip(later_weights[:n_later], later_weights[n_later:]):
        dst[...] = src[...].astype(jnp.bfloat16)

    g = g_ref[...]
    block = x_ref.shape[0] // ROW_BLOCKS
    for r in range(ROW_BLOCKS):
        rows = slice(r * block, (r + 1) * block)
        x = x_ref[rows, :]
        inv_rms = _inv_rms(x)
        proj = jnp.dot((x * g).astype(jnp.bfloat16), w_bf16_ref[...],
                       preferred_element_type=jnp.float32)
        u_ref[rows, :] = (proj[:, :POOL_WIDTH] * inv_rms).astype(jnp.bfloat16)
        q = proj[:, POOL_WIDTH:POOL_WIDTH + NA_WIDTH] * (inv_rms * QUERY_SCALE)
        q_ref[rows, :] = q.astype(jnp.bfloat16)
        k = proj[:, POOL_WIDTH + NA_WIDTH:POOL_WIDTH + 2 * NA_WIDTH] * inv_rms
        k_ref[rows, :] = k.astype(jnp.bfloat16)
        v_ref[rows, :] = (proj[:, POOL_WIDTH + 2 * NA_WIDTH:] * inv_rms).astype(jnp.bfloat16)


def _in_proj(x, g, w_in, later_weights):
    B, T, D = x.shape
    n_t = T // DENSE_TILE
    n_steps = B * n_t
    tile = lambda width: pl.BlockSpec((None, DENSE_TILE, width), lambda b, t: (b, t, 0))
    out = jax.ShapeDtypeStruct((B, T, POOL_WIDTH), jnp.bfloat16)

    def slab(w):
        rows, cols = w.shape
        assert rows % (n_steps * BF16_SUBLANES) == 0
        return pl.BlockSpec((rows // n_steps, cols), lambda b, t: (b * n_t + t, 0))

    results = pl.pallas_call(
        _in_proj_kernel,
        grid=(B, n_t),
        in_specs=[
            tile(D),
            pl.BlockSpec((1, D), lambda b, t: (0, 0)),
            pl.BlockSpec(w_in.shape, lambda b, t: (0, 0), pipeline_mode=pl.Buffered(1)),
        ] + [slab(w) for w in later_weights],
        out_specs=[slab(w) for w in later_weights] + [tile(POOL_WIDTH)] * 4,
        out_shape=[jax.ShapeDtypeStruct(w.shape, jnp.bfloat16) for w in later_weights] + [out] * 4,
        scratch_shapes=[pltpu.VMEM(w_in.shape, jnp.bfloat16)],
        compiler_params=pltpu.CompilerParams(
            dimension_semantics=("arbitrary", "arbitrary"),
            vmem_limit_bytes=V7X_VMEM_LIMIT_BYTES),
        name="in_proj",
    )(x, g, w_in, *later_weights)
    return results[-4:], results[:-4]


def _build_bias_table(rpb_ref, bias_ref):
    n_heads, n_dr, lanes = rpb_ref.shape
    assert lanes == 2 * GRID_W
    qc = lax.broadcasted_iota(jnp.int32, (GRID_W, lanes), 0)
    lane = lax.broadcasted_iota(jnp.int32, (GRID_W, lanes), 1)
    kc = lane % GRID_W
    c0 = jnp.clip(qc - WIN_COLS // 2, 0, GRID_W - WIN_COLS)
    in_window = (kc >= c0) & (kc < c0 + WIN_COLS)
    first_row = lane < GRID_W
    for h in range(n_heads):
        skew = []
        for dr in range(n_dr):
            row = jnp.broadcast_to(rpb_ref[h, dr:dr + 1, :], (GRID_W, lanes))
            skew.append(pltpu.roll(row, 0, axis=1, stride=1, stride_axis=0))
        for p in range(n_dr - 1):
            pair = jnp.where(first_row, skew[p], pltpu.roll(skew[p + 1], GRID_W, axis=1))
            bias_ref[p, h * GRID_W:(h + 1) * GRID_W, :] = jnp.where(in_window, pair * LOG2_E, MASK_VALUE)


def _pooled_minus_token(x, w, first_token, seq_len):
    halo_len = x.shape[0]
    n = halo_len - 2 * HALO
    assert w // 2 <= SUBLANES <= HALO
    t_top = first_token + lax.broadcasted_iota(jnp.int32, (SUBLANES, 1), 0)
    t_bot = t_top + (n - SUBLANES)

    def shifted(a, s):
        return pltpu.roll(a, (-s) % halo_len, axis=0)

    def inv_count(t):
        lo = jnp.maximum(t - w // 2, 0)
        hi = jnp.minimum(t - w // 2 + w, seq_len)
        return 1.0 / (hi - lo).astype(jnp.float32)

    run = x
    for k in range(w.bit_length() - 2):
        run = run + shifted(run, 2 ** k)
    win_sum = (shifted(run, -(w // 2)) + run)[HALO:HALO + n]
    mean = jnp.concatenate([
        win_sum[:SUBLANES] * inv_count(t_top),
        win_sum[SUBLANES:n - SUBLANES] * (1.0 / w),
        win_sum[n - SUBLANES:] * inv_count(t_bot)], axis=0)
    return mean - x[HALO:HALO + n]


def _mixers_kernel(x_ref, u_ref, u_prev_ref, u_next_ref, q_ref, k_ref, v_ref,
                   wpool_ref, pscale_ref, rpb_ref, wout_ref,
                   o_ref, attn_ref, bias_ref):
    tb = pl.program_id(1)
    n_tb = pl.num_programs(1)
    seq_len = n_tb * TOKEN_TILE

    @pl.when((pl.program_id(0) == 0) & (tb == 0))
    def _():
        _build_bias_table(rpb_ref, bias_ref)

    n_rows = n_tb * ROWS_PER_TILE
    kr = min(WIN_ROWS, n_rows)
    lane_head = lax.broadcasted_iota(jnp.int32, (GRID_W, GROUP_WIDTH), 1) // HEAD_DIM
    head_sel = [(lane_head == h).astype(jnp.float32).astype(jnp.bfloat16) > 0
                for h in range(HEADS_PER_GROUP)]

    def attend_row(rl):
        r = tb * ROWS_PER_TILE + rl
        r0 = jnp.clip(r - kr // 2, 0, n_rows - kr)
        dr0 = r0 - r + (WIN_ROWS - 1)
        q_rows = slice(rl * GRID_W, (rl + 1) * GRID_W)
        k_start = pl.multiple_of(r0 * GRID_W, GRID_W)
        for g in range(N_GROUPS):
            gs = slice(g * GROUP_WIDTH, (g + 1) * GROUP_WIDTH)
            q_row = q_ref[q_rows, gs]
            q_bd = jnp.concatenate(
                [jnp.where(head_sel[h], q_row, jnp.zeros_like(q_row))
                 for h in range(HEADS_PER_GROUP)], axis=0)
            k_rows = k_ref[pl.ds(k_start, KEYS_PER_ROW_WINDOW), gs]
            v_rows = v_ref[pl.ds(k_start, KEYS_PER_ROW_WINDOW), gs]
            s = lax.dot_general(q_bd, k_rows, (((1,), (1,)), ((), ())),
                                preferred_element_type=jnp.float32)
            bias = jnp.concatenate(
                [bias_ref[dr0 + 2 * j, gs, :] for j in range(WIN_ROWS // 2)], axis=1)
            s = s + bias
            m = jnp.max(s, axis=-1, keepdims=True)
            p = jnp.exp2(s - m)
            l = jnp.sum(p, axis=-1, keepdims=True)
            o = jnp.dot(p.astype(jnp.bfloat16), v_rows,
                        preferred_element_type=jnp.float32)
            o = o / l
            og = jnp.zeros((GRID_W, GROUP_WIDTH), jnp.float32)
            for h in range(HEADS_PER_GROUP):
                og = jnp.where(lane_head == h, o[h * GRID_W:(h + 1) * GRID_W, :], og)
            attn_ref[q_rows, gs] = og.astype(jnp.bfloat16)

    def pool_pair(pair):
        u_prev = jnp.where(tb > 0, u_prev_ref[...], jnp.zeros_like(u_prev_ref))
        u_next = jnp.where(tb < n_tb - 1, u_next_ref[...], jnp.zeros_like(u_next_ref))
        sl = slice(pair * 2 * POOL_GROUP, (pair + 1) * 2 * POOL_GROUP)
        u_halo = jnp.concatenate([u_prev[:, sl], u_ref[:, sl], u_next[:, sl]], axis=0).astype(jnp.float32)
        pooled = [
            _pooled_minus_token(u_halo[:, i * POOL_GROUP:(i + 1) * POOL_GROUP], POOL_WINDOWS[2 * pair + i],
                                tb * TOKEN_TILE, seq_len).astype(jnp.bfloat16)
            for i in range(2)]
        m = jnp.dot(jnp.concatenate(pooled, axis=1), wpool_ref[pair], preferred_element_type=jnp.float32)
        return (m * pscale_ref[:, sl]).astype(jnp.bfloat16)

    pool_out = jnp.concatenate([pool_pair(0), pool_pair(1)], axis=1)

    for rl in range(ROWS_PER_TILE):
        attend_row(rl)

    y = jnp.dot(jnp.concatenate([pool_out, attn_ref[...]], axis=1), wout_ref[...],
                preferred_element_type=jnp.float32)
    o_ref[...] = x_ref[...] + y


def _mixers(x, u, q, k, v, w_pool_pairs, pool_scale, rpb_rows, w_out_bf16):
    B, T, D = x.shape
    n_heads, n_dr, _ = rpb_rows.shape
    n_t = T // TOKEN_TILE
    halo_per_tile = TOKEN_TILE // HALO
    n_halo_blocks = T // HALO
    tile = lambda width: pl.BlockSpec((None, TOKEN_TILE, width), lambda b, t: (b, t, 0))
    whole_seq = pl.BlockSpec((None, T, NA_WIDTH), lambda b, t: (b, 0, 0))
    const = lambda a: pl.BlockSpec(a.shape, lambda b, t: (0,) * a.ndim)
    return pl.pallas_call(
        _mixers_kernel,
        grid=(B, n_t),
        in_specs=[
            tile(D),
            tile(POOL_WIDTH),
            pl.BlockSpec((None, HALO, POOL_WIDTH),
                         lambda b, t: (b, jnp.maximum(t * halo_per_tile - 1, 0), 0)),
            pl.BlockSpec((None, HALO, POOL_WIDTH),
                         lambda b, t: (b, jnp.minimum((t + 1) * halo_per_tile, n_halo_blocks - 1), 0)),
            tile(NA_WIDTH),
            whole_seq,
            whole_seq,
            const(w_pool_pairs), const(pool_scale), const(rpb_rows), const(w_out_bf16),
        ],
        out_specs=tile(D),
        out_shape=jax.ShapeDtypeStruct((B, T, D), jnp.float32),
        scratch_shapes=[
            pltpu.VMEM((TOKEN_TILE, NA_WIDTH), jnp.bfloat16),
            pltpu.VMEM((n_dr - 1, n_heads * GRID_W, 2 * GRID_W), jnp.float32),
        ],
        compiler_params=pltpu.CompilerParams(
            dimension_semantics=("arbitrary", "arbitrary"),
            vmem_limit_bytes=V7X_VMEM_LIMIT_BYTES),
        name="mixers",
    )(x, u, u, u, q, k, v, w_pool_pairs, pool_scale, rpb_rows, w_out_bf16)


def _mlp_kernel(x_ref, g_ref, wup_ref, wdown_ref, gf_ref, o_ref):
    x = x_ref[...]
    inv_rms = _inv_rms(x)
    h = (x * g_ref[...]).astype(jnp.bfloat16)
    n_chunks = D_FF // FF_CHUNK
    block = x.shape[0] // ROW_BLOCKS
    acc = x
    for c in range(n_chunks - 1):
        cs = slice(c * FF_CHUNK, (c + 1) * FF_CHUNK)
        if c == 0:
            up = jnp.concatenate(
                [jnp.dot(h[r * block:(r + 1) * block], wup_ref[:, cs], preferred_element_type=jnp.float32)
                 for r in range(ROW_BLOCKS)], axis=0) * inv_rms
        else:
            up = jnp.dot(h, wup_ref[:, cs], preferred_element_type=jnp.float32) * inv_rms
        act = jnp.square(jnp.maximum(up, 0.0)).astype(jnp.bfloat16)
        acc = acc + jnp.dot(act, wdown_ref[cs, :], preferred_element_type=jnp.float32)

    cs = slice((n_chunks - 1) * FF_CHUNK, n_chunks * FF_CHUNK)
    for r in range(ROW_BLOCKS):
        rows = slice(r * block, (r + 1) * block)
        up = jnp.dot(h[rows], wup_ref[:, cs], preferred_element_type=jnp.float32) * inv_rms[rows]
        act = jnp.square(jnp.maximum(up, 0.0)).astype(jnp.bfloat16)
        y = acc[rows] + jnp.dot(act, wdown_ref[cs, :], preferred_element_type=jnp.float32)
        o_ref[rows, :] = y * _inv_rms(y) * gf_ref[...]


def _mlp(x, g, w_up_bf16, w_down_bf16, g_final):
    B, T, D = x.shape
    n_t = T // DENSE_TILE
    tile = pl.BlockSpec((None, DENSE_TILE, D), lambda b, t: (b, t, 0))
    const = lambda a: pl.BlockSpec(a.shape, lambda b, t: (0,) * a.ndim,
                                   pipeline_mode=pl.Buffered(1))
    return pl.pallas_call(
        _mlp_kernel,
        grid=(B, n_t),
        in_specs=[tile, const(g), const(w_up_bf16), const(w_down_bf16), const(g_final)],
        out_specs=tile,
        out_shape=jax.ShapeDtypeStruct((B, T, D), jnp.float32),
        compiler_params=pltpu.CompilerParams(
            dimension_semantics=("arbitrary", "arbitrary"),
            vmem_limit_bytes=V7X_VMEM_LIMIT_BYTES),
        name="mlp",
    )(x, g, w_up_bf16, w_down_bf16, g_final)


def _pool_weight_pairs(w_pool_bf16):
    n_groups, cg, _ = w_pool_bf16.shape
    zero = jnp.zeros((cg, cg), w_pool_bf16.dtype)
    pairs = [jnp.block([[w_pool_bf16[2 * p], zero], [zero, w_pool_bf16[2 * p + 1]]])
             for p in range(n_groups // 2)]
    return jnp.stack(pairs)


def _rpb_rows(rpb):
    n_dc = rpb.shape[-1]
    padded = jnp.pad(rpb, ((0, 0), (0, 0), (0, 2 * GRID_W - n_dc)))
    return jnp.roll(padded, -(WIN_COLS - 1), axis=-1)


def kernel(x, norm_mix_g, w_in, w_pool, pool_scale, rpb, w_out, norm_mlp_g, w_up, w_down, final_g):
    bf16 = jnp.bfloat16
    assert w_in.shape[0] == 1, "the final rmsnorm is fused into the (single) layer's MLP call"
    (u, q, k, v), (w_out_bf16, w_up_bf16, w_down_bf16) = _in_proj(
        x, norm_mix_g[0][None, :], w_in[0], (w_out[0], w_up[0], w_down[0]))
    x = _mixers(x, u, q, k, v, _pool_weight_pairs(w_pool[0].astype(bf16)), pool_scale[0][None, :],
                _rpb_rows(rpb[0]), w_out_bf16)
    return _mlp(x, norm_mlp_g[0][None, :], w_up_bf16, w_down_bf16, final_g[None, :])
```

```python
import jax
import jax.numpy as jnp
import numpy as np
from jax import lax
from jax.experimental import pallas as pl
from jax.experimental.pallas import tpu as pltpu

POOL_WIDTH = 512
POOL_WINDOWS = (2, 4, 8, 16)
POOL_GROUP = 128
NA_WIDTH = 512
HEAD_DIM = 64
N_HEADS = 8
GRID_W = 64
WIN_ROWS = 8
WIN_COLS = 16
D_FF = 4096
EPS = 1e-6

HEADS_PER_GROUP = 4
GROUP_WIDTH = HEADS_PER_GROUP * HEAD_DIM
N_GROUPS = N_HEADS // HEADS_PER_GROUP
KEYS_PER_ROW_WINDOW = WIN_ROWS * GRID_W
MASK_VALUE = -1e30
LOG2_E = float(np.log2(np.e))
QUERY_SCALE = HEAD_DIM ** -0.5 * LOG2_E

TOKEN_TILE = 512
ROWS_PER_TILE = TOKEN_TILE // GRID_W
KV_WINDOW_ROWS = ROWS_PER_TILE + WIN_ROWS
DENSE_TILE = 1024
SUBLANES = 8
BF16_SUBLANES = 16
HALO = BF16_SUBLANES
FF_CHUNK = 1024
ROW_BLOCKS = 4

V7X_VMEM_LIMIT_BYTES = 56 * 1024 * 1024


def _inv_rms(x):
    return lax.rsqrt(jnp.mean(jnp.square(x), axis=-1, keepdims=True) + EPS)


def _in_proj_kernel(x_ref, g_ref, w_ref, *refs):
    u_ref, q_ref, k_ref, v_ref, w_bf16_ref = refs[-5:]

    @pl.when((pl.program_id(0) == 0) & (pl.program_id(1) == 0))
    def _():
        w_bf16_ref[...] = w_ref[...].astype(jnp.bfloat16)

    later_weights = refs[:-5]
    n_later = len(later_weights) // 2
    for src, dst in zip(later_weights[:n_later], later_weights[n_later:]):
        dst[...] = src[...].astype(jnp.bfloat16)

    g = g_ref[...]
    block = x_ref.shape[0] // ROW_BLOCKS
    for r in range(ROW_BLOCKS):
        rows = slice(r * block, (r + 1) * block)
        x = x_ref[rows, :]
        inv_rms = _inv_rms(x)
        proj = jnp.dot((x * g).astype(jnp.bfloat16), w_bf16_ref[...],
                       preferred_element_type=jnp.float32)
        u_ref[rows, :] = (proj[:, :POOL_WIDTH] * inv_rms).astype(jnp.bfloat16)
        q = proj[:, POOL_WIDTH:POOL_WIDTH + NA_WIDTH] * (inv_rms * QUERY_SCALE)
        q_ref[rows, :] = q.astype(jnp.bfloat16)
        k = proj[:, POOL_WIDTH + NA_WIDTH:POOL_WIDTH + 2 * NA_WIDTH] * inv_rms
        k_ref[rows, :] = k.astype(jnp.bfloat16)
        v_ref[rows, :] = (proj[:, POOL_WIDTH + 2 * NA_WIDTH:] * inv_rms).astype(jnp.bfloat16)


def _in_proj(x, g, w_in, later_weights):
    B, T, D = x.shape
    n_t = T // DENSE_TILE
    n_steps = B * n_t
    tile = lambda width: pl.BlockSpec((None, DENSE_TILE, width), lambda b, t: (b, t, 0))
    out = jax.ShapeDtypeStruct((B, T, POOL_WIDTH), jnp.bfloat16)

    def slab(w):
        rows, cols = w.shape
        assert rows % (n_steps * BF16_SUBLANES) == 0
        return pl.BlockSpec((rows // n_steps, cols), lambda b, t: (b * n_t + t, 0))

    results = pl.pallas_call(
        _in_proj_kernel,
        grid=(B, n_t),
        in_specs=[
            tile(D),
            pl.BlockSpec((1, D), lambda b, t: (0, 0)),
            pl.BlockSpec(w_in.shape, lambda b, t: (0, 0), pipeline_mode=pl.Buffered(1)),
        ] + [slab(w) for w in later_weights],
        out_specs=[slab(w) for w in later_weights] + [tile(POOL_WIDTH)] * 4,
        out_shape=[jax.ShapeDtypeStruct(w.shape, jnp.bfloat16) for w in later_weights] + [out] * 4,
        scratch_shapes=[pltpu.VMEM(w_in.shape, jnp.bfloat16)],
        compiler_params=pltpu.CompilerParams(
            dimension_semantics=("arbitrary", "arbitrary"),
            vmem_limit_bytes=V7X_VMEM_LIMIT_BYTES),
        name="in_proj",
    )(x, g, w_in, *later_weights)
    return results[-4:], results[:-4]


def _build_bias_table(rpb_ref, bias_ref):
    n_heads, n_dr, lanes = rpb_ref.shape
    assert lanes == 2 * GRID_W
    qc = lax.broadcasted_iota(jnp.int32, (GRID_W, lanes), 0)
    lane = lax.broadcasted_iota(jnp.int32, (GRID_W, lanes), 1)
    kc = lane % GRID_W
    c0 = jnp.clip(qc - WIN_COLS // 2, 0, GRID_W - WIN_COLS)
    in_window = (kc >= c0) & (kc < c0 + WIN_COLS)
    first_row = lane < GRID_W
    for h in range(n_heads):
        skew = []
        for dr in range(n_dr):
            row = jnp.broadcast_to(rpb_ref[h, dr:dr + 1, :], (GRID_W, lanes))
            skew.append(pltpu.roll(row, 0, axis=1, stride=1, stride_axis=0))
        for p in range(n_dr - 1):
            pair = jnp.where(first_row, skew[p], pltpu.roll(skew[p + 1], GRID_W, axis=1))
            bias_ref[p, h * GRID_W:(h + 1) * GRID_W, :] = jnp.where(in_window, pair * LOG2_E, MASK_VALUE)


def _pooled_minus_token(x, w, first_token, seq_len):
    halo_len = x.shape[0]
    n = halo_len - 2 * HALO
    assert w // 2 <= SUBLANES <= HALO
    t_top = first_token + lax.broadcasted_iota(jnp.int32, (SUBLANES, 1), 0)
    t_bot = t_top + (n - SUBLANES)

    def shifted(a, s):
        return pltpu.roll(a, (-s) % halo_len, axis=0)

    def inv_count(t):
        lo = jnp.maximum(t - w // 2, 0)
        hi = jnp.minimum(t - w // 2 + w, seq_len)
        return 1.0 / (hi - lo).astype(jnp.float32)

    run = x
    for k in range(w.bit_length() - 2):
        run = run + shifted(run, 2 ** k)
    win_sum = (shifted(run, -(w // 2)) + run)[HALO:HALO + n]
    mean = jnp.concatenate([
        win_sum[:SUBLANES] * inv_count(t_top),
        win_sum[SUBLANES:n - SUBLANES] * (1.0 / w),
        win_sum[n - SUBLANES:] * inv_count(t_bot)], axis=0)
    return mean - x[HALO:HALO + n]


def _kv_window_start_row(tile_index, n_rows):
    return jnp.clip(tile_index * ROWS_PER_TILE - WIN_ROWS // 2, 0, n_rows - KV_WINDOW_ROWS)


def _mixers_kernel(x_ref, u_ref, u_prev_ref, u_next_ref, q_ref, k_ref, v_ref,
                   wpool_ref, pscale_ref, rpb_ref, wout_ref,
                   o_ref, attn_ref, bias_ref):
    tb = pl.program_id(1)
    n_tb = pl.num_programs(1)
    seq_len = n_tb * TOKEN_TILE

    @pl.when((pl.program_id(0) == 0) & (tb == 0))
    def _():
        _build_bias_table(rpb_ref, bias_ref)

    n_rows = n_tb * ROWS_PER_TILE
    kr = min(WIN_ROWS, n_rows)
    lane_head = lax.broadcasted_iota(jnp.int32, (GRID_W, GROUP_WIDTH), 1) // HEAD_DIM
    head_sel = [(lane_head == h).astype(jnp.float32).astype(jnp.bfloat16) > 0
                for h in range(HEADS_PER_GROUP)]

    def attend_row(rl):
        r = tb * ROWS_PER_TILE + rl
        r0 = jnp.clip(r - kr // 2, 0, n_rows - kr)
        dr0 = r0 - r + (WIN_ROWS - 1)
        q_rows = slice(rl * GRID_W, (rl + 1) * GRID_W)
        k_start = pl.multiple_of((r0 - _kv_window_start_row(tb, n_rows)) * GRID_W, GRID_W)
        for g in range(N_GROUPS):
            gs = slice(g * GROUP_WIDTH, (g + 1) * GROUP_WIDTH)
            q_row = q_ref[q_rows, gs]
            q_bd = jnp.concatenate(
                [jnp.where(head_sel[h], q_row, jnp.zeros_like(q_row))
                 for h in range(HEADS_PER_GROUP)], axis=0)
            k_rows = k_ref[0, pl.ds(k_start, KEYS_PER_ROW_WINDOW), gs]
            v_rows = v_ref[0, pl.ds(k_start, KEYS_PER_ROW_WINDOW), gs]
            s = lax.dot_general(q_bd, k_rows, (((1,), (1,)), ((), ())),
                                preferred_element_type=jnp.float32)
            bias = jnp.concatenate(
                [bias_ref[dr0 + 2 * j, gs, :] for j in range(WIN_ROWS // 2)], axis=1)
            s = s + bias
            m = jnp.max(s, axis=-1, keepdims=True)
            p = jnp.exp2(s - m)
            l = jnp.sum(p, axis=-1, keepdims=True)
            o = jnp.dot(p.astype(jnp.bfloat16), v_rows,
                        preferred_element_type=jnp.float32)
            o = o / l
            og = jnp.zeros((GRID_W, GROUP_WIDTH), jnp.float32)
            for h in range(HEADS_PER_GROUP):
                og = jnp.where(lane_head == h, o[h * GRID_W:(h + 1) * GRID_W, :], og)
            attn_ref[q_rows, gs] = og.astype(jnp.bfloat16)

    def pool_pair(pair):
        u_prev = jnp.where(tb > 0, u_prev_ref[...], jnp.zeros_like(u_prev_ref))
        u_next = jnp.where(tb < n_tb - 1, u_next_ref[...], jnp.zeros_like(u_next_ref))
        sl = slice(pair * 2 * POOL_GROUP, (pair + 1) * 2 * POOL_GROUP)
        u_halo = jnp.concatenate([u_prev[:, sl], u_ref[:, sl], u_next[:, sl]], axis=0).astype(jnp.float32)
        pooled = [
            _pooled_minus_token(u_halo[:, i * POOL_GROUP:(i + 1) * POOL_GROUP], POOL_WINDOWS[2 * pair + i],
                                tb * TOKEN_TILE, seq_len).astype(jnp.bfloat16)
            for i in range(2)]
        m = jnp.dot(jnp.concatenate(pooled, axis=1), wpool_ref[pair], preferred_element_type=jnp.float32)
        return (m * pscale_ref[:, sl]).astype(jnp.bfloat16)

    pool_out = jnp.concatenate([pool_pair(0), pool_pair(1)], axis=1)

    for rl in range(ROWS_PER_TILE):
        attend_row(rl)

    y = jnp.dot(jnp.concatenate([pool_out, attn_ref[...]], axis=1), wout_ref[...],
                preferred_element_type=jnp.float32)
    o_ref[...] = x_ref[...] + y


def _mixers(x, u, q, k, v, w_pool_pairs, pool_scale, rpb_rows, w_out_bf16):
    B, T, D = x.shape
    n_heads, n_dr, _ = rpb_rows.shape
    n_t = T // TOKEN_TILE
    halo_per_tile = TOKEN_TILE // HALO
    n_halo_blocks = T // HALO
    tile = lambda width: pl.BlockSpec((None, TOKEN_TILE, width), lambda b, t: (b, t, 0))
    n_rows = T // GRID_W
    assert n_rows >= KV_WINDOW_ROWS
    kv_window = pl.BlockSpec(
        (pl.Element(1), pl.Element(KV_WINDOW_ROWS * GRID_W), pl.Element(NA_WIDTH)),
        lambda b, t: (b, _kv_window_start_row(t, n_rows) * GRID_W, 0))
    const = lambda a: pl.BlockSpec(a.shape, lambda b, t: (0,) * a.ndim)
    return pl.pallas_call(
        _mixers_kernel,
        grid=(B, n_t),
        in_specs=[
            tile(D),
            tile(POOL_WIDTH),
            pl.BlockSpec((None, HALO, POOL_WIDTH),
                         lambda b, t: (b, jnp.maximum(t * halo_per_tile - 1, 0), 0)),
            pl.BlockSpec((None, HALO, POOL_WIDTH),
                         lambda b, t: (b, jnp.minimum((t + 1) * halo_per_tile, n_halo_blocks - 1), 0)),
            tile(NA_WIDTH),
            kv_window,
            kv_window,
            const(w_pool_pairs), const(pool_scale), const(rpb_rows), const(w_out_bf16),
        ],
        out_specs=tile(D),
        out_shape=jax.ShapeDtypeStruct((B, T, D), jnp.float32),
        scratch_shapes=[
            pltpu.VMEM((TOKEN_TILE, NA_WIDTH), jnp.bfloat16),
            pltpu.VMEM((n_dr - 1, n_heads * GRID_W, 2 * GRID_W), jnp.float32),
        ],
        compiler_params=pltpu.CompilerParams(
            dimension_semantics=("arbitrary", "arbitrary"),
            vmem_limit_bytes=V7X_VMEM_LIMIT_BYTES),
        name="mixers",
    )(x, u, u, u, q, k, v, w_pool_pairs, pool_scale, rpb_rows, w_out_bf16)


def _mlp_kernel(x_ref, g_ref, wup_ref, wdown_ref, gf_ref, o_ref):
    x = x_ref[...]
    inv_rms = _inv_rms(x)
    h = (x * g_ref[...]).astype(jnp.bfloat16)
    n_chunks = D_FF // FF_CHUNK
    block = x.shape[0] // ROW_BLOCKS
    acc = x
    for c in range(n_chunks - 1):
        cs = slice(c * FF_CHUNK, (c + 1) * FF_CHUNK)
        if c == 0:
            up = jnp.concatenate(
                [jnp.dot(h[r * block:(r + 1) * block], wup_ref[:, cs], preferred_element_type=jnp.float32)
                 for r in range(ROW_BLOCKS)], axis=0) * inv_rms
        else:
            up = jnp.dot(h, wup_ref[:, cs], preferred_element_type=jnp.float32) * inv_rms
        act = jnp.square(jnp.maximum(up, 0.0)).astype(jnp.bfloat16)
        acc = acc + jnp.dot(act, wdown_ref[cs, :], preferred_element_type=jnp.float32)

    cs = slice((n_chunks - 1) * FF_CHUNK, n_chunks * FF_CHUNK)
    for r in range(ROW_BLOCKS):
        rows = slice(r * block, (r + 1) * block)
        up = jnp.dot(h[rows], wup_ref[:, cs], preferred_element_type=jnp.float32) * inv_rms[rows]
        act = jnp.square(jnp.maximum(up, 0.0)).astype(jnp.bfloat16)
        y = acc[rows] + jnp.dot(act, wdown_ref[cs, :], preferred_element_type=jnp.float32)
        o_ref[rows, :] = y * _inv_rms(y) * gf_ref[...]


def _mlp(x, g, w_up_bf16, w_down_bf16, g_final):
    B, T, D = x.shape
    n_t = T // DENSE_TILE
    tile = pl.BlockSpec((None, DENSE_TILE, D), lambda b, t: (b, t, 0))
    const = lambda a: pl.BlockSpec(a.shape, lambda b, t: (0,) * a.ndim,
                                   pipeline_mode=pl.Buffered(1))
    return pl.pallas_call(
        _mlp_kernel,
        grid=(B, n_t),
        in_specs=[tile, const(g), const(w_up_bf16), const(w_down_bf16), const(g_final)],
        out_specs=tile,
        out_shape=jax.ShapeDtypeStruct((B, T, D), jnp.float32),
        compiler_params=pltpu.CompilerParams(
            dimension_semantics=("arbitrary", "arbitrary"),
            vmem_limit_bytes=V7X_VMEM_LIMIT_BYTES),
        name="mlp",
    )(x, g, w_up_bf16, w_down_bf16, g_final)


def _pool_weight_pairs(w_pool_bf16):
    n_groups, cg, _ = w_pool_bf16.shape
    zero = jnp.zeros((cg, cg), w_pool_bf16.dtype)
    pairs = [jnp.block([[w_pool_bf16[2 * p], zero], [zero, w_pool_bf16[2 * p + 1]]])
             for p in range(n_groups // 2)]
    return jnp.stack(pairs)


def _rpb_rows(rpb):
    n_dc = rpb.shape[-1]
    padded = jnp.pad(rpb, ((0, 0), (0, 0), (0, 2 * GRID_W - n_dc)))
    return jnp.roll(padded, -(WIN_COLS - 1), axis=-1)


def kernel(x, norm_mix_g, w_in, w_pool, pool_scale, rpb, w_out, norm_mlp_g, w_up, w_down, final_g):
    bf16 = jnp.bfloat16
    assert w_in.shape[0] == 1, "the final rmsnorm is fused into the (single) layer's MLP call"
    (u, q, k, v), (w_out_bf16, w_up_bf16, w_down_bf16) = _in_proj(
        x, norm_mix_g[0][None, :], w_in[0], (w_out[0], w_up[0], w_down[0]))
    x = _mixers(x, u, q, k, v, _pool_weight_pairs(w_pool[0].astype(bf16)), pool_scale[0][None, :],
                _rpb_rows(rpb[0]), w_out_bf16)
    return _mlp(x, norm_mlp_g[0][None, :], w_up_bf16, w_down_bf16, final_g[None, :])
```

```python
import jax
import jax.numpy as jnp
import numpy as np
from jax import lax
from jax.experimental import pallas as pl
from jax.experimental.pallas import tpu as pltpu

POOL_WIDTH = 512
POOL_WINDOWS = (2, 4, 8, 16)
POOL_GROUP = 128
NA_WIDTH = 512
HEAD_DIM = 64
N_HEADS = 8
GRID_W = 64
WIN_ROWS = 8
WIN_COLS = 16
D_FF = 4096
EPS = 1e-6

HEADS_PER_GROUP = 4
GROUP_WIDTH = HEADS_PER_GROUP * HEAD_DIM
N_GROUPS = N_HEADS // HEADS_PER_GROUP
KEYS_PER_ROW_WINDOW = WIN_ROWS * GRID_W
MASK_VALUE = -1e30
LOG2_E = float(np.log2(np.e))
QUERY_SCALE = HEAD_DIM ** -0.5 * LOG2_E

TOKEN_TILE = 1024
ROWS_PER_TILE = TOKEN_TILE // GRID_W
KV_WINDOW_ROWS = ROWS_PER_TILE + WIN_ROWS
DENSE_TILE = 1024
SUBLANES = 8
BF16_SUBLANES = 16
HALO = BF16_SUBLANES
FF_CHUNK = 1024
ROW_BLOCKS = 4

V7X_VMEM_LIMIT_BYTES = 56 * 1024 * 1024


def _inv_rms(x):
    return lax.rsqrt(jnp.mean(jnp.square(x), axis=-1, keepdims=True) + EPS)


def _in_proj_kernel(x_ref, g_ref, w_ref, *refs):
    u_ref, q_ref, k_ref, v_ref, w_bf16_ref = refs[-5:]

    @pl.when((pl.program_id(0) == 0) & (pl.program_id(1) == 0))
    def _():
        w_bf16_ref[...] = w_ref[...].astype(jnp.bfloat16)

    later_weights = refs[:-5]
    n_later = len(later_weights) // 2
    for src, dst in zip(later_weights[:n_later], later_weights[n_later:]):
        dst[...] = src[...].astype(jnp.bfloat16)

    g = g_ref[...]
    block = x_ref.shape[0] // ROW_BLOCKS
    for r in range(ROW_BLOCKS):
        rows = slice(r * block, (r + 1) * block)
        x = x_ref[rows, :]
        inv_rms = _inv_rms(x)
        proj = jnp.dot((x * g).astype(jnp.bfloat16), w_bf16_ref[...],
                       preferred_element_type=jnp.float32)
        u_ref[rows, :] = (proj[:, :POOL_WIDTH] * inv_rms).astype(jnp.bfloat16)
        q = proj[:, POOL_WIDTH:POOL_WIDTH + NA_WIDTH] * (inv_rms * QUERY_SCALE)
        q_ref[rows, :] = q.astype(jnp.bfloat16)
        k = proj[:, POOL_WIDTH + NA_WIDTH:POOL_WIDTH + 2 * NA_WIDTH] * inv_rms
        k_ref[rows, :] = k.astype(jnp.bfloat16)
        v_ref[rows, :] = (proj[:, POOL_WIDTH + 2 * NA_WIDTH:] * inv_rms).astype(jnp.bfloat16)


def _in_proj(x, g, w_in, later_weights):
    B, T, D = x.shape
    n_t = T // DENSE_TILE
    n_steps = B * n_t
    tile = lambda width: pl.BlockSpec((None, DENSE_TILE, width), lambda b, t: (b, t, 0))
    out = jax.ShapeDtypeStruct((B, T, POOL_WIDTH), jnp.bfloat16)

    def slab(w):
        rows, cols = w.shape
        assert rows % (n_steps * BF16_SUBLANES) == 0
        return pl.BlockSpec((rows // n_steps, cols), lambda b, t: (b * n_t + t, 0))

    results = pl.pallas_call(
        _in_proj_kernel,
        grid=(B, n_t),
        in_specs=[
            tile(D),
            pl.BlockSpec((1, D), lambda b, t: (0, 0)),
            pl.BlockSpec(w_in.shape, lambda b, t: (0, 0), pipeline_mode=pl.Buffered(1)),
        ] + [slab(w) for w in later_weights],
        out_specs=[slab(w) for w in later_weights] + [tile(POOL_WIDTH)] * 4,
        out_shape=[jax.ShapeDtypeStruct(w.shape, jnp.bfloat16) for w in later_weights] + [out] * 4,
        scratch_shapes=[pltpu.VMEM(w_in.shape, jnp.bfloat16)],
        compiler_params=pltpu.CompilerParams(
            dimension_semantics=("arbitrary", "arbitrary"),
            vmem_limit_bytes=V7X_VMEM_LIMIT_BYTES),
        name="in_proj",
    )(x, g, w_in, *later_weights)
    return results[-4:], results[:-4]


def _build_bias_table(rpb_ref, bias_ref):
    n_heads, n_dr, lanes = rpb_ref.shape
    assert lanes == 2 * GRID_W
    qc = lax.broadcasted_iota(jnp.int32, (GRID_W, lanes), 0)
    lane = lax.broadcasted_iota(jnp.int32, (GRID_W, lanes), 1)
    kc = lane % GRID_W
    c0 = jnp.clip(qc - WIN_COLS // 2, 0, GRID_W - WIN_COLS)
    in_window = (kc >= c0) & (kc < c0 + WIN_COLS)
    first_row = lane < GRID_W
    for h in range(n_heads):
        skew = []
        for dr in range(n_dr):
            row = jnp.broadcast_to(rpb_ref[h, dr:dr + 1, :], (GRID_W, lanes))
            skew.append(pltpu.roll(row, 0, axis=1, stride=1, stride_axis=0))
        for p in range(n_dr - 1):
            pair = jnp.where(first_row, skew[p], pltpu.roll(skew[p + 1], GRID_W, axis=1))
            bias_ref[p, h * GRID_W:(h + 1) * GRID_W, :] = jnp.where(in_window, pair * LOG2_E, MASK_VALUE)


def _pooled_minus_token(x, w, first_token, seq_len):
    halo_len = x.shape[0]
    n = halo_len - 2 * HALO
    assert w // 2 <= SUBLANES <= HALO
    t_top = first_token + lax.broadcasted_iota(jnp.int32, (SUBLANES, 1), 0)
    t_bot = t_top + (n - SUBLANES)

    def shifted(a, s):
        return pltpu.roll(a, (-s) % halo_len, axis=0)

    def inv_count(t):
        lo = jnp.maximum(t - w // 2, 0)
        hi = jnp.minimum(t - w // 2 + w, seq_len)
        return 1.0 / (hi - lo).astype(jnp.float32)

    run = x
    for k in range(w.bit_length() - 2):
        run = run + shifted(run, 2 ** k)
    win_sum = (shifted(run, -(w // 2)) + run)[HALO:HALO + n]
    mean = jnp.concatenate([
        win_sum[:SUBLANES] * inv_count(t_top),
        win_sum[SUBLANES:n - SUBLANES] * (1.0 / w),
        win_sum[n - SUBLANES:] * inv_count(t_bot)], axis=0)
    return mean - x[HALO:HALO + n]


def _kv_window_start_row(tile_index, n_rows):
    return jnp.clip(tile_index * ROWS_PER_TILE - WIN_ROWS // 2, 0, n_rows - KV_WINDOW_ROWS)


def _mixers_kernel(x_ref, u_ref, u_prev_ref, u_next_ref, q_ref, k_ref, v_ref,
                   wpool_ref, pscale_ref, rpb_ref, wout_ref,
                   o_ref, attn_ref, bias_ref):
    tb = pl.program_id(1)
    n_tb = pl.num_programs(1)
    seq_len = n_tb * TOKEN_TILE

    @pl.when((pl.program_id(0) == 0) & (tb == 0))
    def _():
        _build_bias_table(rpb_ref, bias_ref)

    n_rows = n_tb * ROWS_PER_TILE
    kr = min(WIN_ROWS, n_rows)
    lane_head = lax.broadcasted_iota(jnp.int32, (GRID_W, GROUP_WIDTH), 1) // HEAD_DIM
    head_sel = [(lane_head == h).astype(jnp.float32).astype(jnp.bfloat16) > 0
                for h in range(HEADS_PER_GROUP)]

    def attend_row(rl):
        r = tb * ROWS_PER_TILE + rl
        r0 = jnp.clip(r - kr // 2, 0, n_rows - kr)
        dr0 = r0 - r + (WIN_ROWS - 1)
        q_rows = slice(rl * GRID_W, (rl + 1) * GRID_W)
        k_start = pl.multiple_of((r0 - _kv_window_start_row(tb, n_rows)) * GRID_W, GRID_W)
        for g in range(N_GROUPS):
            gs = slice(g * GROUP_WIDTH, (g + 1) * GROUP_WIDTH)
            q_row = q_ref[q_rows, gs]
            q_bd = jnp.concatenate(
                [jnp.where(head_sel[h], q_row, jnp.zeros_like(q_row))
                 for h in range(HEADS_PER_GROUP)], axis=0)
            k_rows = k_ref[0, pl.ds(k_start, KEYS_PER_ROW_WINDOW), gs]
            v_rows = v_ref[0, pl.ds(k_start, KEYS_PER_ROW_WINDOW), gs]
            s = lax.dot_general(q_bd, k_rows, (((1,), (1,)), ((), ())),
                                preferred_element_type=jnp.float32)
            bias = jnp.concatenate(
                [bias_ref[dr0 + 2 * j, gs, :] for j in range(WIN_ROWS // 2)], axis=1)
            s = s + bias
            m = jnp.max(s, axis=-1, keepdims=True)
            p = jnp.exp2(s - m)
            l = jnp.sum(p, axis=-1, keepdims=True)
            o = jnp.dot(p.astype(jnp.bfloat16), v_rows,
                        preferred_element_type=jnp.float32)
            o = o / l
            og = jnp.zeros((GRID_W, GROUP_WIDTH), jnp.float32)
            for h in range(HEADS_PER_GROUP):
                og = jnp.where(lane_head == h, o[h * GRID_W:(h + 1) * GRID_W, :], og)
            attn_ref[q_rows, gs] = og.astype(jnp.bfloat16)

    def pool_pair(pair):
        u_prev = jnp.where(tb > 0, u_prev_ref[...], jnp.zeros_like(u_prev_ref))
        u_next = jnp.where(tb < n_tb - 1, u_next_ref[...], jnp.zeros_like(u_next_ref))
        sl = slice(pair * 2 * POOL_GROUP, (pair + 1) * 2 * POOL_GROUP)
        u_halo = jnp.concatenate([u_prev[:, sl], u_ref[:, sl], u_next[:, sl]], axis=0).astype(jnp.float32)
        pooled = [
            _pooled_minus_token(u_halo[:, i * POOL_GROUP:(i + 1) * POOL_GROUP], POOL_WINDOWS[2 * pair + i],
                                tb * TOKEN_TILE, seq_len).astype(jnp.bfloat16)
            for i in range(2)]
        m = jnp.dot(jnp.concatenate(pooled, axis=1), wpool_ref[pair], preferred_element_type=jnp.float32)
        return (m * pscale_ref[:, sl]).astype(jnp.bfloat16)

    pool_out = jnp.concatenate([pool_pair(0), pool_pair(1)], axis=1)

    for rl in range(ROWS_PER_TILE):
        attend_row(rl)

    y = jnp.dot(jnp.concatenate([pool_out, attn_ref[...]], axis=1), wout_ref[...],
                preferred_element_type=jnp.float32)
    o_ref[...] = x_ref[...] + y


def _mixers(x, u, q, k, v, w_pool_pairs, pool_scale, rpb_rows, w_out_bf16):
    B, T, D = x.shape
    n_heads, n_dr, _ = rpb_rows.shape
    n_t = T // TOKEN_TILE
    halo_per_tile = TOKEN_TILE // HALO
    n_halo_blocks = T // HALO
    tile = lambda width: pl.BlockSpec((None, TOKEN_TILE, width), lambda b, t: (b, t, 0))
    n_rows = T // GRID_W
    assert n_rows >= KV_WINDOW_ROWS
    kv_window = pl.BlockSpec(
        (pl.Element(1), pl.Element(KV_WINDOW_ROWS * GRID_W), pl.Element(NA_WIDTH)),
        lambda b, t: (b, _kv_window_start_row(t, n_rows) * GRID_W, 0))
    const = lambda a: pl.BlockSpec(a.shape, lambda b, t: (0,) * a.ndim)
    return pl.pallas_call(
        _mixers_kernel,
        grid=(B, n_t),
        in_specs=[
            tile(D),
            tile(POOL_WIDTH),
            pl.BlockSpec((None, HALO, POOL_WIDTH),
                         lambda b, t: (b, jnp.maximum(t * halo_per_tile - 1, 0), 0)),
            pl.BlockSpec((None, HALO, POOL_WIDTH),
                         lambda b, t: (b, jnp.minimum((t + 1) * halo_per_tile, n_halo_blocks - 1), 0)),
            tile(NA_WIDTH),
            kv_window,
            kv_window,
            const(w_pool_pairs), const(pool_scale), const(rpb_rows), const(w_out_bf16),
        ],
        out_specs=tile(D),
        out_shape=jax.ShapeDtypeStruct((B, T, D), jnp.float32),
        scratch_shapes=[
            pltpu.VMEM((TOKEN_TILE, NA_WIDTH), jnp.bfloat16),
            pltpu.VMEM((n_dr - 1, n_heads * GRID_W, 2 * GRID_W), jnp.float32),
        ],
        compiler_params=pltpu.CompilerParams(
            dimension_semantics=("arbitrary", "arbitrary"),
            vmem_limit_bytes=V7X_VMEM_LIMIT_BYTES),
        name="mixers",
    )(x, u, u, u, q, k, v, w_pool_pairs, pool_scale, rpb_rows, w_out_bf16)


def _mlp_kernel(x_ref, g_ref, wup_ref, wdown_ref, gf_ref, o_ref):
    x = x_ref[...]
    inv_rms = _inv_rms(x)
    h = (x * g_ref[...]).astype(jnp.bfloat16)
    n_chunks = D_FF // FF_CHUNK
    block = x.shape[0] // ROW_BLOCKS
    acc = x
    for c in range(n_chunks - 1):
        cs = slice(c * FF_CHUNK, (c + 1) * FF_CHUNK)
        if c == 0:
            up = jnp.concatenate(
                [jnp.dot(h[r * block:(r + 1) * block], wup_ref[:, cs], preferred_element_type=jnp.float32)
                 for r in range(ROW_BLOCKS)], axis=0) * inv_rms
        else:
            up = jnp.dot(h, wup_ref[:, cs], preferred_element_type=jnp.float32) * inv_rms
        act = jnp.square(jnp.maximum(up, 0.0)).astype(jnp.bfloat16)
        acc = acc + jnp.dot(act, wdown_ref[cs, :], preferred_element_type=jnp.float32)

    cs = slice((n_chunks - 1) * FF_CHUNK, n_chunks * FF_CHUNK)
    for r in range(ROW_BLOCKS):
        rows = slice(r * block, (r + 1) * block)
        up = jnp.dot(h[rows], wup_ref[:, cs], preferred_element_type=jnp.float32) * inv_rms[rows]
        act = jnp.square(jnp.maximum(up, 0.0)).astype(jnp.bfloat16)
        y = acc[rows] + jnp.dot(act, wdown_ref[cs, :], preferred_element_type=jnp.float32)
        o_ref[rows, :] = y * _inv_rms(y) * gf_ref[...]


def _mlp(x, g, w_up_bf16, w_down_bf16, g_final):
    B, T, D = x.shape
    n_t = T // DENSE_TILE
    tile = pl.BlockSpec((None, DENSE_TILE, D), lambda b, t: (b, t, 0))
    const = lambda a: pl.BlockSpec(a.shape, lambda b, t: (0,) * a.ndim,
                                   pipeline_mode=pl.Buffered(1))
    return pl.pallas_call(
        _mlp_kernel,
        grid=(B, n_t),
        in_specs=[tile, const(g), const(w_up_bf16), const(w_down_bf16), const(g_final)],
        out_specs=tile,
        out_shape=jax.ShapeDtypeStruct((B, T, D), jnp.float32),
        compiler_params=pltpu.CompilerParams(
            dimension_semantics=("arbitrary", "arbitrary"),
            vmem_limit_bytes=V7X_VMEM_LIMIT_BYTES),
        name="mlp",
    )(x, g, w_up_bf16, w_down_bf16, g_final)


def _pool_weight_pairs(w_pool_bf16):
    n_groups, cg, _ = w_pool_bf16.shape
    zero = jnp.zeros((cg, cg), w_pool_bf16.dtype)
    pairs = [jnp.block([[w_pool_bf16[2 * p], zero], [zero, w_pool_bf16[2 * p + 1]]])
             for p in range(n_groups // 2)]
    return jnp.stack(pairs)


def _rpb_rows(rpb):
    n_dc = rpb.shape[-1]
    padded = jnp.pad(rpb, ((0, 0), (0, 0), (0, 2 * GRID_W - n_dc)))
    return jnp.roll(padded, -(WIN_COLS - 1), axis=-1)


def kernel(x, norm_mix_g, w_in, w_pool, pool_scale, rpb, w_out, norm_mlp_g, w_up, w_down, final_g):
    bf16 = jnp.bfloat16
    assert w_in.shape[0] == 1, "the final rmsnorm is fused into the (single) layer's MLP call"
    (u, q, k, v), (w_out_bf16, w_up_bf16, w_down_bf16) = _in_proj(
        x, norm_mix_g[0][None, :], w_in[0], (w_out[0], w_up[0], w_down[0]))
    x = _mixers(x, u, q, k, v, _pool_weight_pairs(w_pool[0].astype(bf16)), pool_scale[0][None, :],
                _rpb_rows(rpb[0]), w_out_bf16)
    return _mlp(x, norm_mlp_g[0][None, :], w_up_bf16, w_down_bf16, final_g[None, :])
```

```python
import jax
import jax.numpy as jnp
import numpy as np
from jax import lax
from jax.experimental import pallas as pl
from jax.experimental.pallas import tpu as pltpu

POOL_WIDTH = 512
POOL_WINDOWS = (2, 4, 8, 16)
POOL_GROUP = 128
NA_WIDTH = 512
HEAD_DIM = 64
N_HEADS = 8
GRID_W = 64
WIN_ROWS = 8
WIN_COLS = 16
D_FF = 4096
EPS = 1e-6

HEADS_PER_GROUP = 4
GROUP_WIDTH = HEADS_PER_GROUP * HEAD_DIM
N_GROUPS = N_HEADS // HEADS_PER_GROUP
KEYS_PER_ROW_WINDOW = WIN_ROWS * GRID_W
MASK_VALUE = -1e30
LOG2_E = float(np.log2(np.e))
QUERY_SCALE = HEAD_DIM ** -0.5 * LOG2_E

TOKEN_TILE = 1024
ROWS_PER_TILE = TOKEN_TILE // GRID_W
KV_WINDOW_ROWS = ROWS_PER_TILE + WIN_ROWS
DENSE_TILE = 1024
SUBLANES = 8
BF16_SUBLANES = 16
HALO = BF16_SUBLANES
FF_CHUNK = 1024
ROW_BLOCKS = 4

V7X_VMEM_LIMIT_BYTES = 56 * 1024 * 1024


def _inv_rms(x):
    return lax.rsqrt(jnp.mean(jnp.square(x), axis=-1, keepdims=True) + EPS)


def _in_proj_kernel(x_ref, g_ref, w_ref, *refs):
    proj_ref, w_bf16_ref = refs[-2:]

    @pl.when((pl.program_id(0) == 0) & (pl.program_id(1) == 0))
    def _():
        w_bf16_ref[...] = w_ref[...].astype(jnp.bfloat16)

    later_weights = refs[:-2]
    n_later = len(later_weights) // 2
    for src, dst in zip(later_weights[:n_later], later_weights[n_later:]):
        dst[...] = src[...].astype(jnp.bfloat16)

    g = g_ref[...]
    block = x_ref.shape[0] // ROW_BLOCKS
    for r in range(ROW_BLOCKS):
        rows = slice(r * block, (r + 1) * block)
        x = x_ref[rows, :]
        inv_rms = _inv_rms(x)
        proj = jnp.dot((x * g).astype(jnp.bfloat16), w_bf16_ref[...],
                       preferred_element_type=jnp.float32)
        q_cols = slice(POOL_WIDTH, POOL_WIDTH + NA_WIDTH)
        proj_ref[rows, :POOL_WIDTH] = (proj[:, :POOL_WIDTH] * inv_rms).astype(jnp.bfloat16)
        proj_ref[rows, q_cols] = (proj[:, q_cols] * (inv_rms * QUERY_SCALE)).astype(jnp.bfloat16)
        proj_ref[rows, q_cols.stop:] = (proj[:, q_cols.stop:] * inv_rms).astype(jnp.bfloat16)


def _in_proj(x, g, w_in, later_weights):
    B, T, D = x.shape
    n_t = T // DENSE_TILE
    n_steps = B * n_t
    proj_width = w_in.shape[1]
    tile = lambda width: pl.BlockSpec((None, DENSE_TILE, width), lambda b, t: (b, t, 0))
    out = jax.ShapeDtypeStruct((B, T, proj_width), jnp.bfloat16)

    def slab(w):
        rows, cols = w.shape
        assert rows % (n_steps * BF16_SUBLANES) == 0
        return pl.BlockSpec((rows // n_steps, cols), lambda b, t: (b * n_t + t, 0))

    results = pl.pallas_call(
        _in_proj_kernel,
        grid=(B, n_t),
        in_specs=[
            tile(D),
            pl.BlockSpec((1, D), lambda b, t: (0, 0)),
            pl.BlockSpec(w_in.shape, lambda b, t: (0, 0), pipeline_mode=pl.Buffered(1)),
        ] + [slab(w) for w in later_weights],
        out_specs=[slab(w) for w in later_weights] + [tile(proj_width)],
        out_shape=[jax.ShapeDtypeStruct(w.shape, jnp.bfloat16) for w in later_weights] + [out],
        scratch_shapes=[pltpu.VMEM(w_in.shape, jnp.bfloat16)],
        compiler_params=pltpu.CompilerParams(
            dimension_semantics=("arbitrary", "arbitrary"),
            vmem_limit_bytes=V7X_VMEM_LIMIT_BYTES),
        name="in_proj",
    )(x, g, w_in, *later_weights)
    return results[-1], results[:-1]


def _build_bias_table(rpb_ref, bias_ref):
    n_heads, n_dr, lanes = rpb_ref.shape
    assert lanes == 2 * GRID_W
    qc = lax.broadcasted_iota(jnp.int32, (GRID_W, lanes), 0)
    lane = lax.broadcasted_iota(jnp.int32, (GRID_W, lanes), 1)
    kc = lane % GRID_W
    c0 = jnp.clip(qc - WIN_COLS // 2, 0, GRID_W - WIN_COLS)
    in_window = (kc >= c0) & (kc < c0 + WIN_COLS)
    first_row = lane < GRID_W
    for h in range(n_heads):
        skew = []
        for dr in range(n_dr):
            row = jnp.broadcast_to(rpb_ref[h, dr:dr + 1, :], (GRID_W, lanes))
            skew.append(pltpu.roll(row, 0, axis=1, stride=1, stride_axis=0))
        for p in range(n_dr - 1):
            pair = jnp.where(first_row, skew[p], pltpu.roll(skew[p + 1], GRID_W, axis=1))
            bias_ref[p, h * GRID_W:(h + 1) * GRID_W, :] = jnp.where(in_window, pair * LOG2_E, MASK_VALUE)


def _pooled_minus_token(x, w, first_token, seq_len):
    halo_len = x.shape[0]
    n = halo_len - 2 * HALO
    assert w // 2 <= SUBLANES <= HALO
    t_top = first_token + lax.broadcasted_iota(jnp.int32, (SUBLANES, 1), 0)
    t_bot = t_top + (n - SUBLANES)

    def shifted(a, s):
        return pltpu.roll(a, (-s) % halo_len, axis=0)

    def inv_count(t):
        lo = jnp.maximum(t - w // 2, 0)
        hi = jnp.minimum(t - w // 2 + w, seq_len)
        return 1.0 / (hi - lo).astype(jnp.float32)

    run = x
    for k in range(w.bit_length() - 2):
        run = run + shifted(run, 2 ** k)
    win_sum = (shifted(run, -(w // 2)) + run)[HALO:HALO + n]
    mean = jnp.concatenate([
        win_sum[:SUBLANES] * inv_count(t_top),
        win_sum[SUBLANES:n - SUBLANES] * (1.0 / w),
        win_sum[n - SUBLANES:] * inv_count(t_bot)], axis=0)
    return mean - x[HALO:HALO + n]


def _kv_window_start_row(tile_index, n_rows):
    return jnp.clip(tile_index * ROWS_PER_TILE - WIN_ROWS // 2, 0, n_rows - KV_WINDOW_ROWS)


def _mixers_kernel(x_ref, u_ref, u_prev_ref, u_next_ref, q_ref, k_ref, v_ref,
                   wpool_ref, pscale_ref, rpb_ref, wout_ref,
                   o_ref, attn_ref, bias_ref):
    tb = pl.program_id(1)
    n_tb = pl.num_programs(1)
    seq_len = n_tb * TOKEN_TILE

    @pl.when((pl.program_id(0) == 0) & (tb == 0))
    def _():
        _build_bias_table(rpb_ref, bias_ref)

    n_rows = n_tb * ROWS_PER_TILE
    kr = min(WIN_ROWS, n_rows)
    lane_head = lax.broadcasted_iota(jnp.int32, (GRID_W, GROUP_WIDTH), 1) // HEAD_DIM
    head_sel = [(lane_head == h).astype(jnp.float32).astype(jnp.bfloat16) > 0
                for h in range(HEADS_PER_GROUP)]

    def attend_row(rl):
        r = tb * ROWS_PER_TILE + rl
        r0 = jnp.clip(r - kr // 2, 0, n_rows - kr)
        dr0 = r0 - r + (WIN_ROWS - 1)
        q_rows = slice(rl * GRID_W, (rl + 1) * GRID_W)
        k_start = pl.multiple_of((r0 - _kv_window_start_row(tb, n_rows)) * GRID_W, GRID_W)
        for g in range(N_GROUPS):
            gs = slice(g * GROUP_WIDTH, (g + 1) * GROUP_WIDTH)
            q_row = q_ref[q_rows, gs]
            q_bd = jnp.concatenate(
                [jnp.where(head_sel[h], q_row, jnp.zeros_like(q_row))
                 for h in range(HEADS_PER_GROUP)], axis=0)
            k_rows = k_ref[0, pl.ds(k_start, KEYS_PER_ROW_WINDOW), gs]
            v_rows = v_ref[0, pl.ds(k_start, KEYS_PER_ROW_WINDOW), gs]
            s = lax.dot_general(q_bd, k_rows, (((1,), (1,)), ((), ())),
                                preferred_element_type=jnp.float32)
            bias = jnp.concatenate(
                [bias_ref[dr0 + 2 * j, gs, :] for j in range(WIN_ROWS // 2)], axis=1)
            s = s + bias
            m = jnp.max(s, axis=-1, keepdims=True)
            p = jnp.exp2(s - m)
            l = jnp.sum(p, axis=-1, keepdims=True)
            o = jnp.dot(p.astype(jnp.bfloat16), v_rows,
                        preferred_element_type=jnp.float32)
            o = o / l
            og = jnp.zeros((GRID_W, GROUP_WIDTH), jnp.float32)
            for h in range(HEADS_PER_GROUP):
                og = jnp.where(lane_head == h, o[h * GRID_W:(h + 1) * GRID_W, :], og)
            attn_ref[q_rows, gs] = og.astype(jnp.bfloat16)

    def pool_pair(pair):
        u_prev = jnp.where(tb > 0, u_prev_ref[...], jnp.zeros_like(u_prev_ref))
        u_next = jnp.where(tb < n_tb - 1, u_next_ref[...], jnp.zeros_like(u_next_ref))
        sl = slice(pair * 2 * POOL_GROUP, (pair + 1) * 2 * POOL_GROUP)
        u_halo = jnp.concatenate([u_prev[:, sl], u_ref[:, sl], u_next[:, sl]], axis=0).astype(jnp.float32)
        pooled = [
            _pooled_minus_token(u_halo[:, i * POOL_GROUP:(i + 1) * POOL_GROUP], POOL_WINDOWS[2 * pair + i],
                                tb * TOKEN_TILE, seq_len).astype(jnp.bfloat16)
            for i in range(2)]
        m = jnp.dot(jnp.concatenate(pooled, axis=1), wpool_ref[pair], preferred_element_type=jnp.float32)
        return (m * pscale_ref[:, sl]).astype(jnp.bfloat16)

    pool_out = jnp.concatenate([pool_pair(0), pool_pair(1)], axis=1)

    for rl in range(ROWS_PER_TILE):
        attend_row(rl)

    y = jnp.dot(jnp.concatenate([pool_out, attn_ref[...]], axis=1), wout_ref[...],
                preferred_element_type=jnp.float32)
    o_ref[...] = x_ref[...] + y


def _mixers(x, proj, w_pool_pairs, pool_scale, rpb_rows, w_out_bf16):
    B, T, D = x.shape
    n_heads, n_dr, _ = rpb_rows.shape
    n_t = T // TOKEN_TILE
    halo_per_tile = TOKEN_TILE // HALO
    n_halo_blocks = T // HALO
    assert POOL_WIDTH == NA_WIDTH
    u_col, q_col, k_col, v_col = range(4)
    tile = lambda width, col=0: pl.BlockSpec((None, TOKEN_TILE, width), lambda b, t: (b, t, col))
    n_rows = T // GRID_W
    assert n_rows >= KV_WINDOW_ROWS

    def kv_window(col):
        return pl.BlockSpec(
            (pl.Element(1), pl.Element(KV_WINDOW_ROWS * GRID_W), pl.Element(NA_WIDTH)),
            lambda b, t: (b, _kv_window_start_row(t, n_rows) * GRID_W, col * NA_WIDTH))

    const = lambda a: pl.BlockSpec(a.shape, lambda b, t: (0,) * a.ndim)
    return pl.pallas_call(
        _mixers_kernel,
        grid=(B, n_t),
        in_specs=[
            tile(D),
            tile(POOL_WIDTH, u_col),
            pl.BlockSpec((None, HALO, POOL_WIDTH),
                         lambda b, t: (b, jnp.maximum(t * halo_per_tile - 1, 0), u_col)),
            pl.BlockSpec((None, HALO, POOL_WIDTH),
                         lambda b, t: (b, jnp.minimum((t + 1) * halo_per_tile, n_halo_blocks - 1), u_col)),
            tile(NA_WIDTH, q_col),
            kv_window(k_col),
            kv_window(v_col),
            const(w_pool_pairs), const(pool_scale), const(rpb_rows), const(w_out_bf16),
        ],
        out_specs=tile(D),
        out_shape=jax.ShapeDtypeStruct((B, T, D), jnp.float32),
        scratch_shapes=[
            pltpu.VMEM((TOKEN_TILE, NA_WIDTH), jnp.bfloat16),
            pltpu.VMEM((n_dr - 1, n_heads * GRID_W, 2 * GRID_W), jnp.float32),
        ],
        compiler_params=pltpu.CompilerParams(
            dimension_semantics=("arbitrary", "arbitrary"),
            vmem_limit_bytes=V7X_VMEM_LIMIT_BYTES),
        name="mixers",
    )(x, proj, proj, proj, proj, proj, proj, w_pool_pairs, pool_scale, rpb_rows, w_out_bf16)


def _mlp_kernel(x_ref, g_ref, wup_ref, wdown_ref, gf_ref, o_ref):
    x = x_ref[...]
    inv_rms = _inv_rms(x)
    h = (x * g_ref[...]).astype(jnp.bfloat16)
    n_chunks = D_FF // FF_CHUNK
    block = x.shape[0] // ROW_BLOCKS
    acc = x
    for c in range(n_chunks - 1):
        cs = slice(c * FF_CHUNK, (c + 1) * FF_CHUNK)
        if c == 0:
            up = jnp.concatenate(
                [jnp.dot(h[r * block:(r + 1) * block], wup_ref[:, cs], preferred_element_type=jnp.float32)
                 for r in range(ROW_BLOCKS)], axis=0) * inv_rms
        else:
            up = jnp.dot(h, wup_ref[:, cs], preferred_element_type=jnp.float32) * inv_rms
        act = jnp.square(jnp.maximum(up, 0.0)).astype(jnp.bfloat16)
        acc = acc + jnp.dot(act, wdown_ref[cs, :], preferred_element_type=jnp.float32)

    cs = slice((n_chunks - 1) * FF_CHUNK, n_chunks * FF_CHUNK)
    for r in range(ROW_BLOCKS):
        rows = slice(r * block, (r + 1) * block)
        up = jnp.dot(h[rows], wup_ref[:, cs], preferred_element_type=jnp.float32) * inv_rms[rows]
        act = jnp.square(jnp.maximum(up, 0.0)).astype(jnp.bfloat16)
        y = acc[rows] + jnp.dot(act, wdown_ref[cs, :], preferred_element_type=jnp.float32)
        o_ref[rows, :] = y * _inv_rms(y) * gf_ref[...]


def _mlp(x, g, w_up_bf16, w_down_bf16, g_final):
    B, T, D = x.shape
    n_t = T // DENSE_TILE
    tile = pl.BlockSpec((None, DENSE_TILE, D), lambda b, t: (b, t, 0))
    const = lambda a: pl.BlockSpec(a.shape, lambda b, t: (0,) * a.ndim,
                                   pipeline_mode=pl.Buffered(1))
    return pl.pallas_call(
        _mlp_kernel,
        grid=(B, n_t),
        in_specs=[tile, const(g), const(w_up_bf16), const(w_down_bf16), const(g_final)],
        out_specs=tile,
        out_shape=jax.ShapeDtypeStruct((B, T, D), jnp.float32),
        compiler_params=pltpu.CompilerParams(
            dimension_semantics=("arbitrary", "arbitrary"),
            vmem_limit_bytes=V7X_VMEM_LIMIT_BYTES),
        name="mlp",
    )(x, g, w_up_bf16, w_down_bf16, g_final)


def _pool_weight_pairs(w_pool_bf16):
    n_groups, cg, _ = w_pool_bf16.shape
    zero = jnp.zeros((cg, cg), w_pool_bf16.dtype)
    pairs = [jnp.block([[w_pool_bf16[2 * p], zero], [zero, w_pool_bf16[2 * p + 1]]])
             for p in range(n_groups // 2)]
    return jnp.stack(pairs)


def _rpb_rows(rpb):
    n_dc = rpb.shape[-1]
    padded = jnp.pad(rpb, ((0, 0), (0, 0), (0, 2 * GRID_W - n_dc)))
    return jnp.roll(padded, -(WIN_COLS - 1), axis=-1)


def kernel(x, norm_mix_g, w_in, w_pool, pool_scale, rpb, w_out, norm_mlp_g, w_up, w_down, final_g):
    bf16 = jnp.bfloat16
    assert w_in.shape[0] == 1, "the final rmsnorm is fused into the (single) layer's MLP call"
    proj, (w_out_bf16, w_up_bf16, w_down_bf16) = _in_proj(
        x, norm_mix_g[0][None, :], w_in[0], (w_out[0], w_up[0], w_down[0]))
    x = _mixers(x, proj, _pool_weight_pairs(w_pool[0].astype(bf16)), pool_scale[0][None, :],
                _rpb_rows(rpb[0]), w_out_bf16)
    return _mlp(x, norm_mlp_g[0][None, :], w_up_bf16, w_down_bf16, final_g[None, :])
```

```python
import jax
import jax.numpy as jnp
import numpy as np
from jax import lax
from jax.experimental import pallas as pl
from jax.experimental.pallas import tpu as pltpu

POOL_WIDTH = 512
POOL_WINDOWS = (2, 4, 8, 16)
POOL_GROUP = 128
NA_WIDTH = 512
HEAD_DIM = 64
N_HEADS = 8
GRID_W = 64
WIN_ROWS = 8
WIN_COLS = 16
D_FF = 4096
EPS = 1e-6

HEADS_PER_GROUP = 4
GROUP_WIDTH = HEADS_PER_GROUP * HEAD_DIM
N_GROUPS = N_HEADS // HEADS_PER_GROUP
KEYS_PER_ROW_WINDOW = WIN_ROWS * GRID_W
MASK_VALUE = -1e30
LOG2_E = float(np.log2(np.e))
QUERY_SCALE = HEAD_DIM ** -0.5 * LOG2_E

TOKEN_TILE = 1024
ROWS_PER_TILE = TOKEN_TILE // GRID_W
KV_WINDOW_ROWS = ROWS_PER_TILE + WIN_ROWS
DENSE_TILE = 1024
IN_PROJ_TILE = 2048
SUBLANES = 8
BF16_SUBLANES = 16
HALO = BF16_SUBLANES
FF_CHUNK = 1024
ROW_BLOCKS = 4

V7X_VMEM_LIMIT_BYTES = 56 * 1024 * 1024


def _inv_rms(x):
    return lax.rsqrt(jnp.mean(jnp.square(x), axis=-1, keepdims=True) + EPS)


def _in_proj_kernel(x_ref, g_ref, w_ref, *refs):
    u_ref, q_ref, k_ref, v_ref, w_bf16_ref = refs[-5:]

    @pl.when((pl.program_id(0) == 0) & (pl.program_id(1) == 0))
    def _():
        w_bf16_ref[...] = w_ref[...].astype(jnp.bfloat16)

    later_weights = refs[:-5]
    n_later = len(later_weights) // 2
    for src, dst in zip(later_weights[:n_later], later_weights[n_later:]):
        dst[...] = src[...].astype(jnp.bfloat16)

    g = g_ref[...]
    block = x_ref.shape[0] // ROW_BLOCKS
    for r in range(ROW_BLOCKS):
        rows = slice(r * block, (r + 1) * block)
        x = x_ref[rows, :]
        inv_rms = _inv_rms(x)
        proj = jnp.dot((x * g).astype(jnp.bfloat16), w_bf16_ref[...],
                       preferred_element_type=jnp.float32)
        u_ref[rows, :] = (proj[:, :POOL_WIDTH] * inv_rms).astype(jnp.bfloat16)
        q = proj[:, POOL_WIDTH:POOL_WIDTH + NA_WIDTH] * (inv_rms * QUERY_SCALE)
        q_ref[rows, :] = q.astype(jnp.bfloat16)
        k = proj[:, POOL_WIDTH + NA_WIDTH:POOL_WIDTH + 2 * NA_WIDTH] * inv_rms
        k_ref[rows, :] = k.astype(jnp.bfloat16)
        v_ref[rows, :] = (proj[:, POOL_WIDTH + 2 * NA_WIDTH:] * inv_rms).astype(jnp.bfloat16)


def _in_proj(x, g, w_in, later_weights):
    B, T, D = x.shape
    n_t = T // IN_PROJ_TILE
    n_steps = B * n_t
    tile = lambda width: pl.BlockSpec((None, IN_PROJ_TILE, width), lambda b, t: (b, t, 0))
    out = jax.ShapeDtypeStruct((B, T, POOL_WIDTH), jnp.bfloat16)

    def slab(w):
        rows, cols = w.shape
        assert rows % (n_steps * BF16_SUBLANES) == 0
        return pl.BlockSpec((rows // n_steps, cols), lambda b, t: (b * n_t + t, 0))

    results = pl.pallas_call(
        _in_proj_kernel,
        grid=(B, n_t),
        in_specs=[
            tile(D),
            pl.BlockSpec((1, D), lambda b, t: (0, 0)),
            pl.BlockSpec(w_in.shape, lambda b, t: (0, 0), pipeline_mode=pl.Buffered(1)),
        ] + [slab(w) for w in later_weights],
        out_specs=[slab(w) for w in later_weights] + [tile(POOL_WIDTH)] * 4,
        out_shape=[jax.ShapeDtypeStruct(w.shape, jnp.bfloat16) for w in later_weights] + [out] * 4,
        scratch_shapes=[pltpu.VMEM(w_in.shape, jnp.bfloat16)],
        compiler_params=pltpu.CompilerParams(
            dimension_semantics=("arbitrary", "arbitrary"),
            vmem_limit_bytes=V7X_VMEM_LIMIT_BYTES),
        name="in_proj",
    )(x, g, w_in, *later_weights)
    return results[-4:], results[:-4]


def _build_bias_table(rpb_ref, bias_ref):
    n_heads, n_dr, lanes = rpb_ref.shape
    assert lanes == 2 * GRID_W
    qc = lax.broadcasted_iota(jnp.int32, (GRID_W, lanes), 0)
    lane = lax.broadcasted_iota(jnp.int32, (GRID_W, lanes), 1)
    kc = lane % GRID_W
    c0 = jnp.clip(qc - WIN_COLS // 2, 0, GRID_W - WIN_COLS)
    in_window = (kc >= c0) & (kc < c0 + WIN_COLS)
    first_row = lane < GRID_W
    for h in range(n_heads):
        skew = []
        for dr in range(n_dr):
            row = jnp.broadcast_to(rpb_ref[h, dr:dr + 1, :], (GRID_W, lanes))
            skew.append(pltpu.roll(row, 0, axis=1, stride=1, stride_axis=0))
        for p in range(n_dr - 1):
            pair = jnp.where(first_row, skew[p], pltpu.roll(skew[p + 1], GRID_W, axis=1))
            bias_ref[p, h * GRID_W:(h + 1) * GRID_W, :] = jnp.where(in_window, pair * LOG2_E, MASK_VALUE)


def _pooled_minus_token(x, w, first_token, seq_len):
    halo_len = x.shape[0]
    n = halo_len - 2 * HALO
    assert w // 2 <= SUBLANES <= HALO
    t_top = first_token + lax.broadcasted_iota(jnp.int32, (SUBLANES, 1), 0)
    t_bot = t_top + (n - SUBLANES)

    def shifted(a, s):
        return pltpu.roll(a, (-s) % halo_len, axis=0)

    def inv_count(t):
        lo = jnp.maximum(t - w // 2, 0)
        hi = jnp.minimum(t - w // 2 + w, seq_len)
        return 1.0 / (hi - lo).astype(jnp.float32)

    run = x
    for k in range(w.bit_length() - 2):
        run = run + shifted(run, 2 ** k)
    win_sum = (shifted(run, -(w // 2)) + run)[HALO:HALO + n]
    mean = jnp.concatenate([
        win_sum[:SUBLANES] * inv_count(t_top),
        win_sum[SUBLANES:n - SUBLANES] * (1.0 / w),
        win_sum[n - SUBLANES:] * inv_count(t_bot)], axis=0)
    return mean - x[HALO:HALO + n]


def _kv_window_start_row(tile_index, n_rows):
    return jnp.clip(tile_index * ROWS_PER_TILE - WIN_ROWS // 2, 0, n_rows - KV_WINDOW_ROWS)


def _mixers_kernel(x_ref, u_ref, u_prev_ref, u_next_ref, q_ref, k_ref, v_ref,
                   wpool_ref, pscale_ref, rpb_ref, wout_ref,
                   o_ref, attn_ref, bias_ref):
    tb = pl.program_id(1)
    n_tb = pl.num_programs(1)
    seq_len = n_tb * TOKEN_TILE

    @pl.when((pl.program_id(0) == 0) & (tb == 0))
    def _():
        _build_bias_table(rpb_ref, bias_ref)

    n_rows = n_tb * ROWS_PER_TILE
    kr = min(WIN_ROWS, n_rows)
    lane_head = lax.broadcasted_iota(jnp.int32, (GRID_W, GROUP_WIDTH), 1) // HEAD_DIM
    head_sel = [(lane_head == h).astype(jnp.float32).astype(jnp.bfloat16) > 0
                for h in range(HEADS_PER_GROUP)]

    def attend_row(rl):
        r = tb * ROWS_PER_TILE + rl
        r0 = jnp.clip(r - kr // 2, 0, n_rows - kr)
        dr0 = r0 - r + (WIN_ROWS - 1)
        q_rows = slice(rl * GRID_W, (rl + 1) * GRID_W)
        k_start = pl.multiple_of((r0 - _kv_window_start_row(tb, n_rows)) * GRID_W, GRID_W)
        for g in range(N_GROUPS):
            gs = slice(g * GROUP_WIDTH, (g + 1) * GROUP_WIDTH)
            q_row = q_ref[q_rows, gs]
            q_bd = jnp.concatenate(
                [jnp.where(head_sel[h], q_row, jnp.zeros_like(q_row))
                 for h in range(HEADS_PER_GROUP)], axis=0)
            k_rows = k_ref[0, pl.ds(k_start, KEYS_PER_ROW_WINDOW), gs]
            v_rows = v_ref[0, pl.ds(k_start, KEYS_PER_ROW_WINDOW), gs]
            s = lax.dot_general(q_bd, k_rows, (((1,), (1,)), ((), ())),
                                preferred_element_type=jnp.float32)
            bias = jnp.concatenate(
                [bias_ref[dr0 + 2 * j, gs, :] for j in range(WIN_ROWS // 2)], axis=1)
            s = s + bias
            m = jnp.max(s, axis=-1, keepdims=True)
            p = jnp.exp2(s - m)
            l = jnp.sum(p, axis=-1, keepdims=True)
            o = jnp.dot(p.astype(jnp.bfloat16), v_rows,
                        preferred_element_type=jnp.float32)
            o = o / l
            og = jnp.zeros((GRID_W, GROUP_WIDTH), jnp.float32)
            for h in range(HEADS_PER_GROUP):
                og = jnp.where(lane_head == h, o[h * GRID_W:(h + 1) * GRID_W, :], og)
            attn_ref[q_rows, gs] = og.astype(jnp.bfloat16)

    def pool_pair(pair):
        u_prev = jnp.where(tb > 0, u_prev_ref[...], jnp.zeros_like(u_prev_ref))
        u_next = jnp.where(tb < n_tb - 1, u_next_ref[...], jnp.zeros_like(u_next_ref))
        sl = slice(pair * 2 * POOL_GROUP, (pair + 1) * 2 * POOL_GROUP)
        u_halo = jnp.concatenate([u_prev[:, sl], u_ref[:, sl], u_next[:, sl]], axis=0).astype(jnp.float32)
        pooled = [
            _pooled_minus_token(u_halo[:, i * POOL_GROUP:(i + 1) * POOL_GROUP], POOL_WINDOWS[2 * pair + i],
                                tb * TOKEN_TILE, seq_len).astype(jnp.bfloat16)
            for i in range(2)]
        m = jnp.dot(jnp.concatenate(pooled, axis=1), wpool_ref[pair], preferred_element_type=jnp.float32)
        return (m * pscale_ref[:, sl]).astype(jnp.bfloat16)

    pool_out = jnp.concatenate([pool_pair(0), pool_pair(1)], axis=1)

    for rl in range(ROWS_PER_TILE):
        attend_row(rl)

    y = jnp.dot(jnp.concatenate([pool_out, attn_ref[...]], axis=1), wout_ref[...],
                preferred_element_type=jnp.float32)
    o_ref[...] = x_ref[...] + y


def _mixers(x, u, q, k, v, w_pool_pairs, pool_scale, rpb_rows, w_out_bf16):
    B, T, D = x.shape
    n_heads, n_dr, _ = rpb_rows.shape
    n_t = T // TOKEN_TILE
    halo_per_tile = TOKEN_TILE // HALO
    n_halo_blocks = T // HALO
    tile = lambda width: pl.BlockSpec((None, TOKEN_TILE, width), lambda b, t: (b, t, 0))
    n_rows = T // GRID_W
    assert n_rows >= KV_WINDOW_ROWS
    kv_window = pl.BlockSpec(
        (pl.Element(1), pl.Element(KV_WINDOW_ROWS * GRID_W), pl.Element(NA_WIDTH)),
        lambda b, t: (b, _kv_window_start_row(t, n_rows) * GRID_W, 0))
    const = lambda a: pl.BlockSpec(a.shape, lambda b, t: (0,) * a.ndim)
    return pl.pallas_call(
        _mixers_kernel,
        grid=(B, n_t),
        in_specs=[
            tile(D),
            tile(POOL_WIDTH),
            pl.BlockSpec((None, HALO, POOL_WIDTH),
                         lambda b, t: (b, jnp.maximum(t * halo_per_tile - 1, 0), 0)),
            pl.BlockSpec((None, HALO, POOL_WIDTH),
                         lambda b, t: (b, jnp.minimum((t + 1) * halo_per_tile, n_halo_blocks - 1), 0)),
            tile(NA_WIDTH),
            kv_window,
            kv_window,
            const(w_pool_pairs), const(pool_scale), const(rpb_rows), const(w_out_bf16),
        ],
        out_specs=tile(D),
        out_shape=jax.ShapeDtypeStruct((B, T, D), jnp.float32),
        scratch_shapes=[
            pltpu.VMEM((TOKEN_TILE, NA_WIDTH), jnp.bfloat16),
            pltpu.VMEM((n_dr - 1, n_heads * GRID_W, 2 * GRID_W), jnp.float32),
        ],
        compiler_params=pltpu.CompilerParams(
            dimension_semantics=("arbitrary", "arbitrary"),
            vmem_limit_bytes=V7X_VMEM_LIMIT_BYTES),
        name="mixers",
    )(x, u, u, u, q, k, v, w_pool_pairs, pool_scale, rpb_rows, w_out_bf16)


def _mlp_kernel(x_ref, g_ref, wup_ref, wdown_ref, gf_ref, o_ref):
    x = x_ref[...]
    inv_rms = _inv_rms(x)
    h = (x * g_ref[...]).astype(jnp.bfloat16)
    n_chunks = D_FF // FF_CHUNK
    block = x.shape[0] // ROW_BLOCKS
    acc = x
    for c in range(n_chunks - 1):
        cs = slice(c * FF_CHUNK, (c + 1) * FF_CHUNK)
        if c == 0:
            up = jnp.concatenate(
                [jnp.dot(h[r * block:(r + 1) * block], wup_ref[:, cs], preferred_element_type=jnp.float32)
                 for r in range(ROW_BLOCKS)], axis=0) * inv_rms
        else:
            up = jnp.dot(h, wup_ref[:, cs], preferred_element_type=jnp.float32) * inv_rms
        act = jnp.square(jnp.maximum(up, 0.0)).astype(jnp.bfloat16)
        acc = acc + jnp.dot(act, wdown_ref[cs, :], preferred_element_type=jnp.float32)

    cs = slice((n_chunks - 1) * FF_CHUNK, n_chunks * FF_CHUNK)
    for r in range(ROW_BLOCKS):
        rows = slice(r * block, (r + 1) * block)
        up = jnp.dot(h[rows], wup_ref[:, cs], preferred_element_type=jnp.float32) * inv_rms[rows]
        act = jnp.square(jnp.maximum(up, 0.0)).astype(jnp.bfloat16)
        y = acc[rows] + jnp.dot(act, wdown_ref[cs, :], preferred_element_type=jnp.float32)
        o_ref[rows, :] = y * _inv_rms(y) * gf_ref[...]


def _mlp(x, g, w_up_bf16, w_down_bf16, g_final):
    B, T, D = x.shape
    n_t = T // DENSE_TILE
    tile = pl.BlockSpec((None, DENSE_TILE, D), lambda b, t: (b, t, 0))
    const = lambda a: pl.BlockSpec(a.shape, lambda b, t: (0,) * a.ndim,
                                   pipeline_mode=pl.Buffered(1))
    return pl.pallas_call(
        _mlp_kernel,
        grid=(B, n_t),
        in_specs=[tile, const(g), const(w_up_bf16), const(w_down_bf16), const(g_final)],
        out_specs=tile,
        out_shape=jax.ShapeDtypeStruct((B, T, D), jnp.float32),
        compiler_params=pltpu.CompilerParams(
            dimension_semantics=("arbitrary", "arbitrary"),
            vmem_limit_bytes=V7X_VMEM_LIMIT_BYTES),
        name="mlp",
    )(x, g, w_up_bf16, w_down_bf16, g_final)


def _pool_weight_pairs(w_pool_bf16):
    n_groups, cg, _ = w_pool_bf16.shape
    zero = jnp.zeros((cg, cg), w_pool_bf16.dtype)
    pairs = [jnp.block([[w_pool_bf16[2 * p], zero], [zero, w_pool_bf16[2 * p + 1]]])
             for p in range(n_groups // 2)]
    return jnp.stack(pairs)


def _rpb_rows(rpb):
    n_dc = rpb.shape[-1]
    padded = jnp.pad(rpb, ((0, 0), (0, 0), (0, 2 * GRID_W - n_dc)))
    return jnp.roll(padded, -(WIN_COLS - 1), axis=-1)


def kernel(x, norm_mix_g, w_in, w_pool, pool_scale, rpb, w_out, norm_mlp_g, w_up, w_down, final_g):
    bf16 = jnp.bfloat16
    assert w_in.shape[0] == 1, "the final rmsnorm is fused into the (single) layer's MLP call"
    (u, q, k, v), (w_out_bf16, w_up_bf16, w_down_bf16) = _in_proj(
        x, norm_mix_g[0][None, :], w_in[0], (w_out[0], w_up[0], w_down[0]))
    x = _mixers(x, u, q, k, v, _pool_weight_pairs(w_pool[0].astype(bf16)), pool_scale[0][None, :],
                _rpb_rows(rpb[0]), w_out_bf16)
    return _mlp(x, norm_mlp_g[0][None, :], w_up_bf16, w_down_bf16, final_g[None, :])
```

```python
import jax
import jax.numpy as jnp
import numpy as np
from jax import lax
from jax.experimental import pallas as pl
from jax.experimental.pallas import tpu as pltpu

POOL_WIDTH = 512
POOL_WINDOWS = (2, 4, 8, 16)
POOL_GROUP = 128
NA_WIDTH = 512
HEAD_DIM = 64
N_HEADS = 8
GRID_W = 64
WIN_ROWS = 8
WIN_COLS = 16
D_FF = 4096
EPS = 1e-6

HEADS_PER_GROUP = 4
GROUP_WIDTH = HEADS_PER_GROUP * HEAD_DIM
N_GROUPS = N_HEADS // HEADS_PER_GROUP
KEYS_PER_ROW_WINDOW = WIN_ROWS * GRID_W
MASK_VALUE = float("-inf")
LOG2_E = float(np.log2(np.e))
QUERY_SCALE = HEAD_DIM ** -0.5 * LOG2_E

TOKEN_TILE = 1024
ROWS_PER_TILE = TOKEN_TILE // GRID_W
KV_WINDOW_ROWS = ROWS_PER_TILE + WIN_ROWS
DENSE_TILE = 1024
IN_PROJ_TILE = 2048
SUBLANES = 8
BF16_SUBLANES = 16
HALO = BF16_SUBLANES
FF_CHUNK = 1024
ROW_BLOCKS = 4

V7X_VMEM_LIMIT_BYTES = 56 * 1024 * 1024


def _inv_rms(x):
    return lax.rsqrt(jnp.mean(jnp.square(x), axis=-1, keepdims=True) + EPS)


def _in_proj_kernel(x_ref, g_ref, w_ref, *refs):
    u_ref, q_ref, k_ref, v_ref, w_bf16_ref = refs[-5:]

    @pl.when((pl.program_id(0) == 0) & (pl.program_id(1) == 0))
    def _():
        w_bf16_ref[...] = w_ref[...].astype(jnp.bfloat16)

    later_weights = refs[:-5]
    n_later = len(later_weights) // 2
    for src, dst in zip(later_weights[:n_later], later_weights[n_later:]):
        dst[...] = src[...].astype(jnp.bfloat16)

    g = g_ref[...]
    block = x_ref.shape[0] // ROW_BLOCKS
    for r in range(ROW_BLOCKS):
        rows = slice(r * block, (r + 1) * block)
        x = x_ref[rows, :]
        inv_rms = _inv_rms(x)
        proj = jnp.dot((x * g).astype(jnp.bfloat16), w_bf16_ref[...],
                       preferred_element_type=jnp.float32)
        u_ref[rows, :] = (proj[:, :POOL_WIDTH] * inv_rms).astype(jnp.bfloat16)
        q = proj[:, POOL_WIDTH:POOL_WIDTH + NA_WIDTH] * (inv_rms * QUERY_SCALE)
        q_ref[rows, :] = q.astype(jnp.bfloat16)
        k = proj[:, POOL_WIDTH + NA_WIDTH:POOL_WIDTH + 2 * NA_WIDTH] * inv_rms
        k_ref[rows, :] = k.astype(jnp.bfloat16)
        v_ref[rows, :] = (proj[:, POOL_WIDTH + 2 * NA_WIDTH:] * inv_rms).astype(jnp.bfloat16)


def _in_proj(x, g, w_in, later_weights):
    B, T, D = x.shape
    n_t = T // IN_PROJ_TILE
    n_steps = B * n_t
    tile = lambda width: pl.BlockSpec((None, IN_PROJ_TILE, width), lambda b, t: (b, t, 0))
    out = jax.ShapeDtypeStruct((B, T, POOL_WIDTH), jnp.bfloat16)

    def slab(w):
        rows, cols = w.shape
        assert rows % (n_steps * BF16_SUBLANES) == 0
        return pl.BlockSpec((rows // n_steps, cols), lambda b, t: (b * n_t + t, 0))

    results = pl.pallas_call(
        _in_proj_kernel,
        grid=(B, n_t),
        in_specs=[
            tile(D),
            pl.BlockSpec((1, D), lambda b, t: (0, 0)),
            pl.BlockSpec(w_in.shape, lambda b, t: (0, 0), pipeline_mode=pl.Buffered(1)),
        ] + [slab(w) for w in later_weights],
        out_specs=[slab(w) for w in later_weights] + [tile(POOL_WIDTH)] * 4,
        out_shape=[jax.ShapeDtypeStruct(w.shape, jnp.bfloat16) for w in later_weights] + [out] * 4,
        scratch_shapes=[pltpu.VMEM(w_in.shape, jnp.bfloat16)],
        compiler_params=pltpu.CompilerParams(
            dimension_semantics=("arbitrary", "arbitrary"),
            vmem_limit_bytes=V7X_VMEM_LIMIT_BYTES),
        name="in_proj",
    )(x, g, w_in, *later_weights)
    return results[-4:], results[:-4]


def _build_bias_table(rpb_ref, bias_ref):
    n_heads, n_dr, lanes = rpb_ref.shape
    assert lanes == 2 * GRID_W
    qc = lax.broadcasted_iota(jnp.int32, (GRID_W, lanes), 0)
    lane = lax.broadcasted_iota(jnp.int32, (GRID_W, lanes), 1)
    kc = lane % GRID_W
    c0 = jnp.clip(qc - WIN_COLS // 2, 0, GRID_W - WIN_COLS)
    in_window = (kc >= c0) & (kc < c0 + WIN_COLS)
    first_row = lane < GRID_W
    for h in range(n_heads):
        skew = []
        for dr in range(n_dr):
            row = jnp.broadcast_to(rpb_ref[h, dr:dr + 1, :], (GRID_W, lanes))
            skew.append(pltpu.roll(row, 0, axis=1, stride=1, stride_axis=0))
        for p in range(n_dr - 1):
            pair = jnp.where(first_row, skew[p], pltpu.roll(skew[p + 1], GRID_W, axis=1))
            bias_ref[p, h * GRID_W:(h + 1) * GRID_W, :] = jnp.where(in_window, pair * LOG2_E, MASK_VALUE)


def _pooled_minus_token(x, w, first_token, seq_len):
    halo_len = x.shape[0]
    n = halo_len - 2 * HALO
    assert w // 2 <= SUBLANES <= HALO
    t_top = first_token + lax.broadcasted_iota(jnp.int32, (SUBLANES, 1), 0)
    t_bot = t_top + (n - SUBLANES)

    def shifted(a, s):
        return pltpu.roll(a, (-s) % halo_len, axis=0)

    def inv_count(t):
        lo = jnp.maximum(t - w // 2, 0)
        hi = jnp.minimum(t - w // 2 + w, seq_len)
        return 1.0 / (hi - lo).astype(jnp.float32)

    run = x
    for k in range(w.bit_length() - 2):
        run = run + shifted(run, 2 ** k)
    win_sum = (shifted(run, -(w // 2)) + run)[HALO:HALO + n]
    mean = jnp.concatenate([
        win_sum[:SUBLANES] * inv_count(t_top),
        win_sum[SUBLANES:n - SUBLANES] * (1.0 / w),
        win_sum[n - SUBLANES:] * inv_count(t_bot)], axis=0)
    return mean - x[HALO:HALO + n]


def _kv_window_start_row(tile_index, n_rows):
    return jnp.clip(tile_index * ROWS_PER_TILE - WIN_ROWS // 2, 0, n_rows - KV_WINDOW_ROWS)


def _mixers_kernel(x_ref, u_ref, u_prev_ref, u_next_ref, q_ref, k_ref, v_ref,
                   wpool_ref, pscale_ref, rpb_ref, wout_ref,
                   o_ref, attn_ref, bias_ref):
    tb = pl.program_id(1)
    n_tb = pl.num_programs(1)
    seq_len = n_tb * TOKEN_TILE

    @pl.when((pl.program_id(0) == 0) & (tb == 0))
    def _():
        _build_bias_table(rpb_ref, bias_ref)

    n_rows = n_tb * ROWS_PER_TILE
    kr = min(WIN_ROWS, n_rows)
    lane_head = lax.broadcasted_iota(jnp.int32, (GRID_W, GROUP_WIDTH), 1) // HEAD_DIM
    head_sel = [(lane_head == h).astype(jnp.float32).astype(jnp.bfloat16) > 0
                for h in range(HEADS_PER_GROUP)]

    def attend_row(rl):
        r = tb * ROWS_PER_TILE + rl
        r0 = jnp.clip(r - kr // 2, 0, n_rows - kr)
        dr0 = r0 - r + (WIN_ROWS - 1)
        q_rows = slice(rl * GRID_W, (rl + 1) * GRID_W)
        k_start = pl.multiple_of((r0 - _kv_window_start_row(tb, n_rows)) * GRID_W, GRID_W)
        for g in range(N_GROUPS):
            gs = slice(g * GROUP_WIDTH, (g + 1) * GROUP_WIDTH)
            q_row = q_ref[q_rows, gs]
            q_bd = jnp.concatenate(
                [jnp.where(head_sel[h], q_row, jnp.zeros_like(q_row))
                 for h in range(HEADS_PER_GROUP)], axis=0)
            k_rows = k_ref[0, pl.ds(k_start, KEYS_PER_ROW_WINDOW), gs]
            v_rows = v_ref[0, pl.ds(k_start, KEYS_PER_ROW_WINDOW), gs]
            s = lax.dot_general(q_bd, k_rows, (((1,), (1,)), ((), ())),
                                preferred_element_type=jnp.float32)
            bias = jnp.concatenate(
                [bias_ref[dr0 + 2 * j, gs, :] for j in range(WIN_ROWS // 2)], axis=1)
            s = s + bias
            m = jnp.max(s, axis=-1, keepdims=True)
            p = jnp.exp2(s - m)
            l = jnp.sum(p, axis=-1, keepdims=True)
            o = jnp.dot(p.astype(jnp.bfloat16), v_rows,
                        preferred_element_type=jnp.float32)
            o = o / l
            og = jnp.zeros((GRID_W, GROUP_WIDTH), jnp.float32)
            for h in range(HEADS_PER_GROUP):
                og = jnp.where(lane_head == h, o[h * GRID_W:(h + 1) * GRID_W, :], og)
            attn_ref[q_rows, gs] = og.astype(jnp.bfloat16)

    def pool_pair(pair):
        u_prev = jnp.where(tb > 0, u_prev_ref[...], jnp.zeros_like(u_prev_ref))
        u_next = jnp.where(tb < n_tb - 1, u_next_ref[...], jnp.zeros_like(u_next_ref))
        sl = slice(pair * 2 * POOL_GROUP, (pair + 1) * 2 * POOL_GROUP)
        u_halo = jnp.concatenate([u_prev[:, sl], u_ref[:, sl], u_next[:, sl]], axis=0).astype(jnp.float32)
        pooled = [
            _pooled_minus_token(u_halo[:, i * POOL_GROUP:(i + 1) * POOL_GROUP], POOL_WINDOWS[2 * pair + i],
                                tb * TOKEN_TILE, seq_len).astype(jnp.bfloat16)
            for i in range(2)]
        m = jnp.dot(jnp.concatenate(pooled, axis=1), wpool_ref[pair], preferred_element_type=jnp.float32)
        return (m * pscale_ref[:, sl]).astype(jnp.bfloat16)

    pool_out = jnp.concatenate([pool_pair(0), pool_pair(1)], axis=1)

    for rl in range(ROWS_PER_TILE):
        attend_row(rl)

    y = jnp.dot(jnp.concatenate([pool_out, attn_ref[...]], axis=1), wout_ref[...],
                preferred_element_type=jnp.float32)
    o_ref[...] = x_ref[...] + y


def _mixers(x, u, q, k, v, w_pool_pairs, pool_scale, rpb_rows, w_out_bf16):
    B, T, D = x.shape
    n_heads, n_dr, _ = rpb_rows.shape
    n_t = T // TOKEN_TILE
    halo_per_tile = TOKEN_TILE // HALO
    n_halo_blocks = T // HALO
    tile = lambda width: pl.BlockSpec((None, TOKEN_TILE, width), lambda b, t: (b, t, 0))
    n_rows = T // GRID_W
    assert n_rows >= KV_WINDOW_ROWS
    kv_window = pl.BlockSpec(
        (pl.Element(1), pl.Element(KV_WINDOW_ROWS * GRID_W), pl.Element(NA_WIDTH)),
        lambda b, t: (b, _kv_window_start_row(t, n_rows) * GRID_W, 0))
    const = lambda a: pl.BlockSpec(a.shape, lambda b, t: (0,) * a.ndim)
    return pl.pallas_call(
        _mixers_kernel,
        grid=(B, n_t),
        in_specs=[
            tile(D),
            tile(POOL_WIDTH),
            pl.BlockSpec((None, HALO, POOL_WIDTH),
                         lambda b, t: (b, jnp.maximum(t * halo_per_tile - 1, 0), 0)),
            pl.BlockSpec((None, HALO, POOL_WIDTH),
                         lambda b, t: (b, jnp.minimum((t + 1) * halo_per_tile, n_halo_blocks - 1), 0)),
            tile(NA_WIDTH),
            kv_window,
            kv_window,
            const(w_pool_pairs), const(pool_scale), const(rpb_rows), const(w_out_bf16),
        ],
        out_specs=tile(D),
        out_shape=jax.ShapeDtypeStruct((B, T, D), jnp.float32),
        scratch_shapes=[
            pltpu.VMEM((TOKEN_TILE, NA_WIDTH), jnp.bfloat16),
            pltpu.VMEM((n_dr - 1, n_heads * GRID_W, 2 * GRID_W), jnp.float32),
        ],
        compiler_params=pltpu.CompilerParams(
            dimension_semantics=("arbitrary", "arbitrary"),
            vmem_limit_bytes=V7X_VMEM_LIMIT_BYTES),
        name="mixers",
    )(x, u, u, u, q, k, v, w_pool_pairs, pool_scale, rpb_rows, w_out_bf16)


def _mlp_kernel(x_ref, g_ref, wup_ref, wdown_ref, gf_ref, o_ref):
    x = x_ref[...]
    inv_rms = _inv_rms(x)
    h = (x * g_ref[...]).astype(jnp.bfloat16)
    n_chunks = D_FF // FF_CHUNK
    block = x.shape[0] // ROW_BLOCKS
    acc = x
    for c in range(n_chunks - 1):
        cs = slice(c * FF_CHUNK, (c + 1) * FF_CHUNK)
        if c == 0:
            up = jnp.concatenate(
                [jnp.dot(h[r * block:(r + 1) * block], wup_ref[:, cs], preferred_element_type=jnp.float32)
                 for r in range(ROW_BLOCKS)], axis=0) * inv_rms
        else:
            up = jnp.dot(h, wup_ref[:, cs], preferred_element_type=jnp.float32) * inv_rms
        act = jnp.square(jnp.maximum(up, 0.0)).astype(jnp.bfloat16)
        acc = acc + jnp.dot(act, wdown_ref[cs, :], preferred_element_type=jnp.float32)

    cs = slice((n_chunks - 1) * FF_CHUNK, n_chunks * FF_CHUNK)
    for r in range(ROW_BLOCKS):
        rows = slice(r * block, (r + 1) * block)
        up = jnp.dot(h[rows], wup_ref[:, cs], preferred_element_type=jnp.float32) * inv_rms[rows]
        act = jnp.square(jnp.maximum(up, 0.0)).astype(jnp.bfloat16)
        y = acc[rows] + jnp.dot(act, wdown_ref[cs, :], preferred_element_type=jnp.float32)
        o_ref[rows, :] = y * _inv_rms(y) * gf_ref[...]


def _mlp(x, g, w_up_bf16, w_down_bf16, g_final):
    B, T, D = x.shape
    n_t = T // DENSE_TILE
    tile = pl.BlockSpec((None, DENSE_TILE, D), lambda b, t: (b, t, 0))
    const = lambda a: pl.BlockSpec(a.shape, lambda b, t: (0,) * a.ndim,
                                   pipeline_mode=pl.Buffered(1))
    return pl.pallas_call(
        _mlp_kernel,
        grid=(B, n_t),
        in_specs=[tile, const(g), const(w_up_bf16), const(w_down_bf16), const(g_final)],
        out_specs=tile,
        out_shape=jax.ShapeDtypeStruct((B, T, D), jnp.float32),
        compiler_params=pltpu.CompilerParams(
            dimension_semantics=("arbitrary", "arbitrary"),
            vmem_limit_bytes=V7X_VMEM_LIMIT_BYTES),
        name="mlp",
    )(x, g, w_up_bf16, w_down_bf16, g_final)


def _pool_weight_pairs(w_pool_bf16):
    n_groups, cg, _ = w_pool_bf16.shape
    zero = jnp.zeros((cg, cg), w_pool_bf16.dtype)
    pairs = [jnp.block([[w_pool_bf16[2 * p], zero], [zero, w_pool_bf16[2 * p + 1]]])
             for p in range(n_groups // 2)]
    return jnp.stack(pairs)


def _rpb_rows(rpb):
    n_dc = rpb.shape[-1]
    padded = jnp.pad(rpb, ((0, 0), (0, 0), (0, 2 * GRID_W - n_dc)))
    return jnp.roll(padded, -(WIN_COLS - 1), axis=-1)


def kernel(x, norm_mix_g, w_in, w_pool, pool_scale, rpb, w_out, norm_mlp_g, w_up, w_down, final_g):
    bf16 = jnp.bfloat16
    assert w_in.shape[0] == 1, "the final rmsnorm is fused into the (single) layer's MLP call"
    (u, q, k, v), (w_out_bf16, w_up_bf16, w_down_bf16) = _in_proj(
        x, norm_mix_g[0][None, :], w_in[0], (w_out[0], w_up[0], w_down[0]))
    x = _mixers(x, u, q, k, v, _pool_weight_pairs(w_pool[0].astype(bf16)), pool_scale[0][None, :],
                _rpb_rows(rpb[0]), w_out_bf16)
    return _mlp(x, norm_mlp_g[0][None, :], w_up_bf16, w_down_bf16, final_g[None, :])
```
